```python
import math, functools
import jax, jax.numpy as jnp
from jax import lax
import numpy as np

D_MODEL = 4096
BATCH = 1
SEQ = 8192
DEPTH = 1
DEC_BATCH = 128
DEC_SEQ = 8
PAST_LEN = 8192
PAGE_SIZE = 128

HEAD_DIM = 128
ATTN_DIM = D_MODEL // 2
N_HEADS = ATTN_DIM // HEAD_DIM
N_KV_HEADS = 4
GQA_GROUP = N_HEADS // N_KV_HEADS
KV_DIM = N_KV_HEADS * HEAD_DIM
CONV_DIM = D_MODEL - ATTN_DIM
CONV_GROUPS = CONV_DIM // HEAD_DIM
CONV_WIDTH = 3
WINDOW = 128
BLOCK = 128
N_BUCKETS = 32
MAX_DISTANCE = 128
D_FF = 11008
N_MOD = 9
PROJ_DIM = ATTN_DIM + 2 * KV_DIM + 3 * CONV_DIM
EPS = 1e-6
NEG = -1e30

kernel_name = "hymba_swa_sink_shortconv_macaron_step"


def _rmsnorm(x, g):
    xf = x.astype(jnp.float32)
    y = xf * lax.rsqrt(jnp.mean(xf * xf, axis=-1, keepdims=True) + EPS)
    return (y * g.astype(jnp.float32)).astype(x.dtype)


def _modulate(h, shift, scale):
    return h * (1 + scale) + shift


def _swiglu(h, w1, w3, w2):
    return (jax.nn.silu(h @ w1) * (h @ w3)) @ w2


def _t5_bucket(dist):
    n = jnp.maximum(dist, 0)
    max_exact = N_BUCKETS // 2
    nf = jnp.maximum(n, 1).astype(jnp.float32)
    large = max_exact + (jnp.log(nf / max_exact) / math.log(MAX_DISTANCE / max_exact)
                         * (N_BUCKETS - max_exact)).astype(jnp.int32)
    large = jnp.minimum(large, N_BUCKETS - 1)
    return jnp.where(n < max_exact, n, large)


def _rel_bias(dist, table):
    b = jnp.moveaxis(table[_t5_bucket(dist)].astype(jnp.float32), -1, 0)
    return b.reshape(N_KV_HEADS, GQA_GROUP, *dist.shape)


def _sink_attention(q, k, v, bias, valid, sinks):
    s = jnp.einsum('...qhgd,...khd->...hgqk', q, k).astype(jnp.float32) * (HEAD_DIM ** -0.5) + bias
    s = jnp.where(valid, s, NEG)
    sk = sinks.astype(jnp.float32).reshape(N_KV_HEADS, GQA_GROUP, 1, 1)
    m = jnp.maximum(jnp.max(s, axis=-1, keepdims=True), sk)
    p = jnp.exp(s - m)
    denom = jnp.sum(p, axis=-1, keepdims=True) + jnp.exp(sk - m)
    p = (p / denom).astype(v.dtype)
    return jnp.einsum('...hgqk,...khd->...qhgd', p, v)


def _split(proj):
    cuts = [ATTN_DIM, ATTN_DIM + KV_DIM, ATTN_DIM + 2 * KV_DIM,
            ATTN_DIM + 2 * KV_DIM + CONV_DIM, ATTN_DIM + 2 * KV_DIM + 2 * CONV_DIM]
    return jnp.split(proj, cuts, axis=-1)


def _causal_conv(ucat, conv_w, s_len):
    y = ucat[:, 0:s_len] * conv_w[0]
    for j in range(1, CONV_WIDTH):
        y = y + ucat[:, j:j + s_len] * conv_w[j]
    return y


def _prompt_mixer(proj, sinks, conv_w, rel_bias):
    bp, s_len, _ = proj.shape
    q, k, v, gb, gc, hc = _split(proj)
    nb = s_len // BLOCK
    q = q.reshape(bp, nb, BLOCK, N_KV_HEADS, GQA_GROUP, HEAD_DIM)
    k = k.reshape(bp, s_len, N_KV_HEADS, HEAD_DIM)
    v = v.reshape(bp, s_len, N_KV_HEADS, HEAD_DIM)
    pad = ((0, 0), (1, 0), (0, 0), (0, 0), (0, 0))
    kb = k.reshape(bp, nb, BLOCK, N_KV_HEADS, HEAD_DIM)
    vb = v.reshape(bp, nb, BLOCK, N_KV_HEADS, HEAD_DIM)
    kk = jnp.concatenate([jnp.pad(kb, pad)[:, :-1], kb], axis=2)
    vv = jnp.concatenate([jnp.pad(vb, pad)[:, :-1], vb], axis=2)
    r = jnp.arange(BLOCK)
    cidx = jnp.arange(2 * BLOCK)
    dist = BLOCK + r[:, None] - cidx[None, :]
    kpos = (jnp.arange(nb)[:, None] - 1) * BLOCK + cidx[None, :]
    valid = ((dist >= 0) & (dist <= WINDOW))[None] & (kpos >= 0)[:, None, :]
    valid = valid[:, None, None]
    o = _sink_attention(q, kk, vv, _rel_bias(dist, rel_bias), valid, sinks)
    o = o.reshape(bp, s_len, ATTN_DIM)
    u = gc * hc
    upad = jnp.pad(u, ((0, 0), (CONV_WIDTH - 1, 0), (0, 0)))
    yc = gb * _causal_conv(upad, conv_w, s_len)
    wb = min(WINDOW, s_len)
    state = (k[:, s_len - wb:], v[:, s_len - wb:], u[:, s_len - (CONV_WIDTH - 1):])
    return jnp.concatenate([o, yc], axis=-1), state


def _sample_mixer(proj, ck, cv, cconv, sinks, conv_w, rel_bias):
    bd, s_len, _ = proj.shape
    wb = ck.shape[1]
    q, k, v, gb, gc, hc = _split(proj)
    q = q.reshape(bd, s_len, N_KV_HEADS, GQA_GROUP, HEAD_DIM)
    k = k.reshape(bd, s_len, N_KV_HEADS, HEAD_DIM)
    v = v.reshape(bd, s_len, N_KV_HEADS, HEAD_DIM)
    kk = jnp.concatenate([ck.astype(k.dtype), k], axis=1)
    vv = jnp.concatenate([cv.astype(v.dtype), v], axis=1)
    qpos = PAST_LEN + jnp.arange(s_len)
    kpos = jnp.concatenate([PAST_LEN - wb + jnp.arange(wb), qpos])
    dist = qpos[:, None] - kpos[None, :]
    valid = (dist >= 0) & (dist <= WINDOW) & (kpos >= 0)[None, :]
    o = _sink_attention(q, kk, vv, _rel_bias(dist, rel_bias), valid, sinks)
    o = o.reshape(bd, s_len, ATTN_DIM)
    u = gc * hc
    ucat = jnp.concatenate([cconv.astype(u.dtype), u], axis=1)
    yc = gb * _causal_conv(ucat, conv_w, s_len)
    state = (kk[:, -wb:], vv[:, -wb:], ucat[:, -(CONV_WIDTH - 1):])
    return jnp.concatenate([o, yc], axis=-1), state


def _layer(x, c, mix_fn, g1, w1a, w3a, w2a, g_mix, w_in, w_out, g2, w1b, w3b, w2b, w_ada, b_ada):
    mod = (jax.nn.silu(c) @ w_ada + b_ada).reshape(c.shape[0], N_MOD, D_MODEL)[:, :, None, :]
    h = _modulate(_rmsnorm(x, g1), mod[:, 0], mod[:, 1])
    x = x + 0.5 * mod[:, 2] * _swiglu(h, w1a, w3a, w2a)
    h = _modulate(_rmsnorm(x, g_mix), mod[:, 3], mod[:, 4])
    mixed, state = mix_fn(h @ w_in)
    x = x + mod[:, 5] * (mixed @ w_out)
    h = _modulate(_rmsnorm(x, g2), mod[:, 6], mod[:, 7])
    x = x + 0.5 * mod[:, 8] * _swiglu(h, w1b, w3b, w2b)
    return x, state


def setup_inputs(seed: int = 0) -> dict:
    key = jax.random.key(seed)
    ks = jax.random.split(key, 32)
    f32 = jnp.float32
    wb = min(WINDOW, PAST_LEN)
    nrm = lambda k, shape, s: jax.random.normal(k, shape, f32) * s
    gain = lambda k: 1.0 + 0.05 * jax.random.normal(k, (DEPTH, D_MODEL), f32)
    sd, sf = D_MODEL ** -0.5, D_FF ** -0.5
    return {
        "x_prompt": nrm(ks[0], (BATCH, SEQ, D_MODEL), 1.0),
        "x_sample": nrm(ks[1], (DEC_BATCH, DEC_SEQ, D_MODEL), 1.0),
        "c_prompt": nrm(ks[2], (BATCH, D_MODEL), 1.0),
        "c_sample": nrm(ks[3], (DEC_BATCH, D_MODEL), 1.0),
        "cache_k": nrm(ks[4], (DEPTH, DEC_BATCH, wb, N_KV_HEADS, HEAD_DIM), 1.0),
        "cache_v": nrm(ks[5], (DEPTH, DEC_BATCH, wb, N_KV_HEADS, HEAD_DIM), 1.0),
        "state_conv": nrm(ks[6], (DEPTH, DEC_BATCH, CONV_WIDTH - 1, CONV_DIM), 1.0),
        "rel_bias": nrm(ks[7], (N_BUCKETS, N_HEADS), 0.5),
        "g_ffn1": gain(ks[8]),
        "w1_ffn1": nrm(ks[9], (DEPTH, D_MODEL, D_FF), sd),
        "w3_ffn1": nrm(ks[10], (DEPTH, D_MODEL, D_FF), sd),
        "w2_ffn1": nrm(ks[11], (DEPTH, D_FF, D_MODEL), sf),
        "g_mix": gain(ks[12]),
        "w_in": nrm(ks[13], (DEPTH, D_MODEL, PROJ_DIM), sd),
        "sinks": nrm(ks[14], (DEPTH, N_HEADS), 1.0),
        "conv_w": nrm(ks[15], (DEPTH, CONV_WIDTH, CONV_DIM), CONV_WIDTH ** -0.5),
        "w_out": nrm(ks[16], (DEPTH, D_MODEL, D_MODEL), sd),
        "g_ffn2": gain(ks[17]),
        "w1_ffn2": nrm(ks[18], (DEPTH, D_MODEL, D_FF), sd),
        "w3_ffn2": nrm(ks[19], (DEPTH, D_MODEL, D_FF), sd),
        "w2_ffn2": nrm(ks[20], (DEPTH, D_FF, D_MODEL), sf),
        "w_ada": nrm(ks[21], (DEPTH, D_MODEL, N_MOD * D_MODEL), 0.5 * sd),
        "b_ada": nrm(ks[22], (DEPTH, N_MOD * D_MODEL), 0.02),
        "g_final": 1.0 + 0.05 * jax.random.normal(ks[23], (D_MODEL,), f32),
    }


def reference(x_prompt, x_sample, c_prompt, c_sample, cache_k, cache_v, state_conv, rel_bias,
              g_ffn1, w1_ffn1, w3_ffn1, w2_ffn1, g_mix, w_in, sinks, conv_w, w_out,
              g_ffn2, w1_ffn2, w3_ffn2, w2_ffn2, w_ada, b_ada, g_final):
    xp, xs = x_prompt, x_sample
    kp, vp, cp, ksn, vsn, csn = [], [], [], [], [], []
    for l in range(DEPTH):
        w = (g_ffn1[l], w1_ffn1[l], w3_ffn1[l], w2_ffn1[l], g_mix[l], w_in[l], w_out[l],
             g_ffn2[l], w1_ffn2[l], w3_ffn2[l], w2_ffn2[l], w_ada[l], b_ada[l])
        pmix = functools.partial(_prompt_mixer, sinks=sinks[l], conv_w=conv_w[l], rel_bias=rel_bias)
        smix = functools.partial(_sample_mixer, ck=cache_k[l], cv=cache_v[l], cconv=state_conv[l],
                                 sinks=sinks[l], conv_w=conv_w[l], rel_bias=rel_bias)
        xp, sp = _layer(xp, c_prompt, pmix, *w)
        xs, ss = _layer(xs, c_sample, smix, *w)
        kp.append(sp[0]); vp.append(sp[1]); cp.append(sp[2])
        ksn.append(ss[0]); vsn.append(ss[1]); csn.append(ss[2])
    y_prompt = _rmsnorm(xp, g_final)
    y_sample = _rmsnorm(xs, g_final)
    k_win_prompt, v_win_prompt, conv_prompt = jnp.stack(kp), jnp.stack(vp), jnp.stack(cp)
    k_win_sample, v_win_sample, conv_sample = jnp.stack(ksn), jnp.stack(vsn), jnp.stack(csn)
    return (y_prompt, y_sample, k_win_prompt, v_win_prompt, conv_prompt, k_win_sample, v_win_sample, conv_sample)
```

```python
import functools
import math

import numpy as np
import jax
import jax.numpy as jnp
from jax import lax
from jax.experimental import pallas as pl
from jax.experimental.pallas import tpu as pltpu

F32 = jnp.float32
BF16 = jnp.bfloat16

D_MODEL = 4096
SEQ = 8192
DEC_BATCH = 128
DEC_SEQ = 8
M_PROMPT = SEQ
M_SAMPLE = DEC_BATCH * DEC_SEQ
M_ALL = M_PROMPT + M_SAMPLE
HEAD_DIM = 128
ATTN_DIM = D_MODEL // 2
N_HEADS = ATTN_DIM // HEAD_DIM
N_KV_HEADS = 4
GQA_GROUP = N_HEADS // N_KV_HEADS
KV_DIM = N_KV_HEADS * HEAD_DIM
CONV_DIM = D_MODEL - ATTN_DIM
CONV_WIDTH = 3
WINDOW = 128
BLOCK = 128
N_BUCKETS = 32
MAX_DISTANCE = 128
D_FF = 11008
N_MOD = 9
PROJ_DIM = ATTN_DIM + 2 * KV_DIM + 3 * CONV_DIM
EPS = 1e-6
NEG = -1e30
SCALE = HEAD_DIM ** -0.5

ROW_GROUP = 128
VMEM_LIMIT_BYTES = 56 * 1024 * 1024


def _params(semantics):
    return pltpu.CompilerParams(dimension_semantics=semantics,
                                vmem_limit_bytes=VMEM_LIMIT_BYTES)


def _slab_index(row_block, rows_per_block):
    return (row_block < M_PROMPT // rows_per_block).astype(jnp.int32)


def _bucket_map():
    r = np.arange(BLOCK)[:, None]
    c = np.arange(2 * BLOCK)[None, :]
    n = np.maximum(BLOCK + r - c, 0)
    max_exact = N_BUCKETS // 2
    nf = np.maximum(n, 1).astype(np.float32)
    large = max_exact + (np.log(nf / max_exact) / math.log(MAX_DISTANCE / max_exact)
                         * (N_BUCKETS - max_exact)).astype(np.int32)
    large = np.minimum(large, N_BUCKETS - 1)
    return np.where(n < max_exact, n, large).astype(np.int32)


def _ada_kernel(c_ref, w_ref, b_ref, o_ref):
    c = c_ref[...]
    a = (c / (1.0 + jnp.exp(-c))).astype(BF16)
    o_ref[...] = jnp.dot(a, w_ref[...].astype(BF16),
                         preferred_element_type=F32) + b_ref[...]


def _ada(c_all, w_ada, b_ada, tn=512):
    rows = c_all.shape[0]
    n = w_ada.shape[1]
    return pl.pallas_call(
        _ada_kernel,
        out_shape=jax.ShapeDtypeStruct((rows, n), F32),
        grid=(n // tn,),
        in_specs=[pl.BlockSpec((rows, D_MODEL), lambda j: (0, 0)),
                  pl.BlockSpec((D_MODEL, tn), lambda j: (0, j)),
                  pl.BlockSpec((1, tn), lambda j: (0, j))],
        out_specs=pl.BlockSpec((rows, tn), lambda j: (0, j)),
        compiler_params=_params(("parallel",)),
        name="ada",
    )(c_all, w_ada, b_ada.reshape(1, n))


def _norm_mod_kernel(x_ref, g_ref, sh_ref, sc_ref, o_ref):
    x = x_ref[...]
    tr, d = x.shape
    y = x * lax.rsqrt(jnp.mean(x * x, axis=-1, keepdims=True) + EPS) * g_ref[...]
    y = y.reshape(tr // ROW_GROUP, ROW_GROUP, d)
    o = y * (1.0 + sc_ref[...])[None] + sh_ref[...][None]
    o_ref[...] = o.reshape(tr, d).astype(BF16)


def _norm_mod(x, g, mod, k_shift, k_scale, tr=256):
    m = x.shape[0]
    slab = lambda k: pl.BlockSpec((ROW_GROUP, D_MODEL),
                                  lambda i: (_slab_index(i, tr), k))
    return pl.pallas_call(
        _norm_mod_kernel,
        out_shape=jax.ShapeDtypeStruct((m, D_MODEL), BF16),
        grid=(m // tr,),
        in_specs=[pl.BlockSpec((tr, D_MODEL), lambda i: (i, 0)),
                  pl.BlockSpec((1, D_MODEL), lambda i: (0, 0)),
                  slab(k_shift), slab(k_scale)],
        out_specs=pl.BlockSpec((tr, D_MODEL), lambda i: (i, 0)),
        compiler_params=_params(("parallel",)),
        name="norm_mod",
    )(x, g.reshape(1, D_MODEL), mod, mod)


def _final_norm_kernel(x_ref, g_ref, o_ref):
    x = x_ref[...]
    o_ref[...] = x * lax.rsqrt(jnp.mean(x * x, axis=-1, keepdims=True) + EPS) * g_ref[...]


def _final_norm(x, g, row0, rows, tr=256):
    return pl.pallas_call(
        _final_norm_kernel,
        out_shape=jax.ShapeDtypeStruct((rows, D_MODEL), F32),
        grid=(rows // tr,),
        in_specs=[pl.BlockSpec((tr, D_MODEL), lambda i: (i + row0 // tr, 0)),
                  pl.BlockSpec((1, D_MODEL), lambda i: (0, 0))],
        out_specs=pl.BlockSpec((tr, D_MODEL), lambda i: (i, 0)),
        compiler_params=_params(("parallel",)),
        name="final_norm",
    )(x, g.reshape(1, D_MODEL))


def _mm_plain_kernel(x_ref, w_ref, o_ref):
    o_ref[...] = jnp.dot(x_ref[...], w_ref[...], preferred_element_type=F32)


def _mm_plain(x, w, tm=1024, tn=512):
    m, k = x.shape
    n = w.shape[1]
    return pl.pallas_call(
        _mm_plain_kernel,
        out_shape=jax.ShapeDtypeStruct((m, n), F32),
        grid=(m // tm, n // tn),
        in_specs=[pl.BlockSpec((tm, k), lambda i, j: (i, 0)),
                  pl.BlockSpec((k, tn), lambda i, j: (0, j))],
        out_specs=pl.BlockSpec((tm, tn), lambda i, j: (i, j)),
        compiler_params=_params(("parallel", "parallel")),
        name="mm_in",
    )(x, w)


def _mm_swiglu_kernel(x_ref, w1_ref, w3_ref, o_ref):
    x = x_ref[...]
    a = jnp.dot(x, w1_ref[...], preferred_element_type=F32)
    b = jnp.dot(x, w3_ref[...], preferred_element_type=F32)
    o_ref[...] = (a / (1.0 + jnp.exp(-a)) * b).astype(BF16)


def _mm_swiglu(x, w1, w3, tm=1024, tn=256):
    m, k = x.shape
    n = w1.shape[1]
    return pl.pallas_call(
        _mm_swiglu_kernel,
        out_shape=jax.ShapeDtypeStruct((m, n), BF16),
        grid=(m // tm, n // tn),
        in_specs=[pl.BlockSpec((tm, k), lambda i, j: (i, 0)),
                  pl.BlockSpec((k, tn), lambda i, j: (0, j)),
                  pl.BlockSpec((k, tn), lambda i, j: (0, j))],
        out_specs=pl.BlockSpec((tm, tn), lambda i, j: (i, j)),
        compiler_params=_params(("parallel", "parallel")),
        name="mm_swiglu",
    )(x, w1, w3)


def _mm_res_kernel(x_ref, w_ref, res_ref, gate_ref, o_ref, *, coef):
    acc = jnp.dot(x_ref[...], w_ref[...], preferred_element_type=F32)
    tm, tn = acc.shape
    gate = coef * gate_ref[...]
    y = acc.reshape(tm // ROW_GROUP, ROW_GROUP, tn) * gate[None]
    o_ref[...] = res_ref[...] + y.reshape(tm, tn)


def _mm_res(x, w, res, mod, k_gate, coef, tm, tn, name):
    m, k = x.shape
    n = w.shape[1]
    gate_blocks = D_MODEL // tn
    return pl.pallas_call(
        functools.partial(_mm_res_kernel, coef=coef),
        out_shape=jax.ShapeDtypeStruct((m, n), F32),
        grid=(m // tm, n // tn),
        in_specs=[pl.BlockSpec((tm, k), lambda i, j: (i, 0)),
                  pl.BlockSpec((k, tn), lambda i, j: (0, j)),
                  pl.BlockSpec((tm, tn), lambda i, j: (i, j)),
                  pl.BlockSpec((ROW_GROUP, tn),
                               lambda i, j: (_slab_index(i, tm), k_gate * gate_blocks + j))],
        out_specs=pl.BlockSpec((tm, tn), lambda i, j: (i, j)),
        compiler_params=_params(("parallel", "parallel")),
        name=name,
    )(x, w, res, mod)


def _build_bias(tab_ref, bucket, bias_scr):
    for h in range(N_HEADS):
        acc = jnp.zeros(bucket.shape, F32)
        for b in range(N_BUCKETS):
            acc = jnp.where(bucket == b, tab_ref[b * N_HEADS + h], acc)
        bias_scr[h] = acc


def _valid_mask(rows, has_prev):
    r = lax.broadcasted_iota(jnp.int32, (rows, 2 * BLOCK), 0)
    c = lax.broadcasted_iota(jnp.int32, (rows, 2 * BLOCK), 1)
    dist = BLOCK + r - c
    return (dist >= 0) & (dist <= WINDOW) & ((c >= BLOCK) | has_prev)


def _softmax_with_sink(s, sink):
    m = jnp.maximum(jnp.max(s, axis=-1, keepdims=True), sink)
    p = jnp.exp(s - m)
    denom = jnp.sum(p, axis=-1, keepdims=True) + jnp.exp(sink - m)
    return p / denom


def _attn_prompt_kernel(tab_ref, sink_ref, bucket_ref, q_ref, kc_ref, kp_ref, vc_ref, vp_ref,
                        o_ref, bias_scr):
    i = pl.program_id(0)

    @pl.when(i == 0)
    def _():
        _build_bias(tab_ref, bucket_ref[...], bias_scr)

    valid = _valid_mask(BLOCK, i > 0)
    for g in range(N_KV_HEADS):
        kv = slice(g * HEAD_DIM, (g + 1) * HEAD_DIM)
        heads = [g * GQA_GROUP + j for j in range(GQA_GROUP)]
        qs = jnp.concatenate(
            [q_ref[:, h * HEAD_DIM:(h + 1) * HEAD_DIM] for h in heads], axis=0).astype(BF16)
        kk = jnp.concatenate([kp_ref[:, kv], kc_ref[:, kv]], axis=0).astype(BF16)
        vv = jnp.concatenate([vp_ref[:, kv], vc_ref[:, kv]], axis=0).astype(BF16)
        s = lax.dot_general(qs, kk, (((1,), (1,)), ((), ())), preferred_element_type=F32)
        ps = []
        for j, h in enumerate(heads):
            sj = s[j * BLOCK:(j + 1) * BLOCK] * SCALE + bias_scr[h]
            sj = jnp.where(valid, sj, NEG)
            ps.append(_softmax_with_sink(sj, sink_ref[h]))
        p = jnp.concatenate(ps, axis=0).astype(BF16)
        o = jnp.dot(p, vv, preferred_element_type=F32)
        for j, h in enumerate(heads):
            o_ref[:, h * HEAD_DIM:(h + 1) * HEAD_DIM] = o[j * BLOCK:(j + 1) * BLOCK].astype(BF16)


def _attn_prompt(proj, table, sinks, bucket):
    nb = M_PROMPT // BLOCK
    q_blk = 0
    k_blk = ATTN_DIM // KV_DIM
    v_blk = k_blk + 1
    prev = lambda i: jnp.maximum(i - 1, 0)
    smem = pl.BlockSpec(memory_space=pltpu.SMEM)
    return pl.pallas_call(
        _attn_prompt_kernel,
        out_shape=jax.ShapeDtypeStruct((M_PROMPT, ATTN_DIM), BF16),
        grid=(nb,),
        in_specs=[smem, smem,
                  pl.BlockSpec((BLOCK, 2 * BLOCK), lambda i: (0, 0)),
                  pl.BlockSpec((BLOCK, ATTN_DIM), lambda i: (i, q_blk)),
                  pl.BlockSpec((BLOCK, KV_DIM), lambda i: (i, k_blk)),
                  pl.BlockSpec((BLOCK, KV_DIM), lambda i: (prev(i), k_blk)),
                  pl.BlockSpec((BLOCK, KV_DIM), lambda i: (i, v_blk)),
                  pl.BlockSpec((BLOCK, KV_DIM), lambda i: (prev(i), v_blk))],
        out_specs=pl.BlockSpec((BLOCK, ATTN_DIM), lambda i: (i, 0)),
        scratch_shapes=[pltpu.VMEM((N_HEADS, BLOCK, 2 * BLOCK), F32)],
        compiler_params=_params(("arbitrary",)),
        name="attn_prompt",
    )(table, sinks, bucket, proj, proj, proj, proj, proj)


def _attn_sample_kernel(tab_ref, sink_ref, bucket_ref, q_ref, kn_ref, vn_ref, ck_ref, cv_ref,
                        o_ref, kw_ref, vw_ref, bias_scr):
    i = pl.program_id(0)
    g_seq = q_ref.shape[0]
    wb = ck_ref.shape[1]

    @pl.when(i == 0)
    def _():
        _build_bias(tab_ref, bucket_ref[0:DEC_SEQ, :], bias_scr)

    valid = _valid_mask(DEC_SEQ, True)[None]
    pad = jnp.zeros((g_seq, 2 * BLOCK - wb - DEC_SEQ, HEAD_DIM), F32)
    for g in range(N_KV_HEADS):
        kv = slice(g * HEAD_DIM, (g + 1) * HEAD_DIM)
        heads = [g * GQA_GROUP + j for j in range(GQA_GROUP)]
        qs = jnp.concatenate(
            [q_ref[:, :, h * HEAD_DIM:(h + 1) * HEAD_DIM] for h in heads], axis=1).astype(BF16)
        kk = jnp.concatenate([ck_ref[:, :, kv], kn_ref[:, :, kv], pad], axis=1).astype(BF16)
        vv = jnp.concatenate([cv_ref[:, :, kv], vn_ref[:, :, kv], pad], axis=1).astype(BF16)
        s = jnp.einsum('gqd,gkd->gqk', qs, kk, preferred_element_type=F32)
        ps = []
        for j, h in enumerate(heads):
            sj = s[:, j * DEC_SEQ:(j + 1) * DEC_SEQ] * SCALE + bias_scr[h][None]
            sj = jnp.where(valid, sj, NEG)
            ps.append(_softmax_with_sink(sj, sink_ref[h]))
        p = jnp.concatenate(ps, axis=1).astype(BF16)
        o = jnp.einsum('gqk,gkd->gqd', p, vv, preferred_element_type=F32)
        for j, h in enumerate(heads):
            o_ref[:, :, h * HEAD_DIM:(h + 1) * HEAD_DIM] = (
                o[:, j * DEC_SEQ:(j + 1) * DEC_SEQ].astype(BF16))
    kw_ref[:, 0:wb - DEC_SEQ, :] = ck_ref[:, DEC_SEQ:wb, :]
    kw_ref[:, wb - DEC_SEQ:wb, :] = kn_ref[...]
    vw_ref[:, 0:wb - DEC_SEQ, :] = cv_ref[:, DEC_SEQ:wb, :]
    vw_ref[:, wb - DEC_SEQ:wb, :] = vn_ref[...]


def _attn_sample(q, kn, vn, ck, cv, table, sinks, bucket, g_seq=8):
    wb = ck.shape[1]
    smem = pl.BlockSpec(memory_space=pltpu.SMEM)
    blk = lambda r, c: pl.BlockSpec((g_seq, r, c), lambda i: (i, 0, 0))
    return pl.pallas_call(
        _attn_sample_kernel,
        out_shape=(jax.ShapeDtypeStruct((DEC_BATCH, DEC_SEQ, ATTN_DIM), BF16),
                   jax.ShapeDtypeStruct((DEC_BATCH, wb, KV_DIM), F32),
                   jax.ShapeDtypeStruct((DEC_BATCH, wb, KV_DIM), F32)),
        grid=(DEC_BATCH // g_seq,),
        in_specs=[smem, smem,
                  pl.BlockSpec((BLOCK, 2 * BLOCK), lambda i: (0, 0)),
                  blk(DEC_SEQ, ATTN_DIM), blk(DEC_SEQ, KV_DIM), blk(DEC_SEQ, KV_DIM),
                  blk(wb, KV_DIM), blk(wb, KV_DIM)],
        out_specs=(blk(DEC_SEQ, ATTN_DIM), blk(wb, KV_DIM), blk(wb, KV_DIM)),
        scratch_shapes=[pltpu.VMEM((N_HEADS, DEC_SEQ, 2 * BLOCK), F32)],
        compiler_params=_params(("arbitrary",)),
        name="attn_sample",
    )(table, sinks, bucket, q, kn, vn, ck, cv)


CONV_COLS = 1024
GATE_B_BLK = (ATTN_DIM + 2 * KV_DIM) // CONV_COLS
GATE_C_BLK = GATE_B_BLK + CONV_DIM // CONV_COLS
CONV_H_BLK = GATE_C_BLK + CONV_DIM // CONV_COLS
TAIL_ROWS = 8


def _conv_prompt_kernel(gb_ref, gc_ref, hc_ref, w_ref, yc_ref, tail_ref, scr):
    i = pl.program_id(1)
    tr = gb_ref.shape[0]

    @pl.when(i == 0)
    def _():
        scr[0:TAIL_ROWS, :] = jnp.zeros((TAIL_ROWS, scr.shape[1]), F32)

    @pl.when(i > 0)
    def _():
        scr[0:TAIL_ROWS, :] = scr[tr:tr + TAIL_ROWS, :]

    u = gc_ref[...] * hc_ref[...]
    scr[TAIL_ROWS:TAIL_ROWS + tr, :] = u
    u1 = scr[TAIL_ROWS - 1:TAIL_ROWS - 1 + tr, :]
    u2 = scr[TAIL_ROWS - 2:TAIL_ROWS - 2 + tr, :]
    y = u2 * w_ref[0:1, :] + u1 * w_ref[1:2, :] + u * w_ref[2:3, :]
    yc_ref[...] = (gb_ref[...] * y).astype(BF16)
    tail_ref[...] = u[tr - TAIL_ROWS:tr, :]


def _conv_prompt(proj, conv_w, tr=512):
    nc = CONV_DIM // CONV_COLS
    col = lambda base: pl.BlockSpec((tr, CONV_COLS), lambda c, i: (i, base + c))
    return pl.pallas_call(
        _conv_prompt_kernel,
        out_shape=(jax.ShapeDtypeStruct((M_PROMPT, CONV_DIM), BF16),
                   jax.ShapeDtypeStruct((TAIL_ROWS, CONV_DIM), F32)),
        grid=(nc, M_PROMPT // tr),
        in_specs=[col(GATE_B_BLK), col(GATE_C_BLK), col(CONV_H_BLK),
                  pl.BlockSpec((CONV_WIDTH, CONV_COLS), lambda c, i: (0, c))],
        out_specs=(pl.BlockSpec((tr, CONV_COLS), lambda c, i: (i, c)),
                   pl.BlockSpec((TAIL_ROWS, CONV_COLS), lambda c, i: (0, c))),
        scratch_shapes=[pltpu.VMEM((tr + TAIL_ROWS, CONV_COLS), F32)],
        compiler_params=_params(("arbitrary", "arbitrary")),
        name="conv_prompt",
    )(proj, proj, proj, conv_w)


def _conv_sample_kernel(gb_ref, gc_ref, hc_ref, w_ref, st_ref, yc_ref, tail_ref):
    rows, cols = gb_ref.shape
    shape = (DEC_SEQ, DEC_BATCH, cols)
    u = (gc_ref[...] * hc_ref[...]).reshape(shape)
    st = st_ref[...]
    u1 = jnp.concatenate([st[1:2], u[:DEC_SEQ - 1]], axis=0)
    u2 = jnp.concatenate([st[0:2], u[:DEC_SEQ - 2]], axis=0)
    y = u2 * w_ref[0:1, :][None] + u1 * w_ref[1:2, :][None] + u * w_ref[2:3, :][None]
    yc_ref[...] = (gb_ref[...].reshape(shape) * y).reshape(rows, cols).astype(BF16)
    tail_ref[...] = u[DEC_SEQ - (CONV_WIDTH - 1):]


def _conv_sample(proj, conv_w, state):
    nc = CONV_DIM // CONV_COLS
    row_blk = M_PROMPT // M_SAMPLE
    col = lambda base: pl.BlockSpec((M_SAMPLE, CONV_COLS), lambda c: (row_blk, base + c))
    return pl.pallas_call(
        _conv_sample_kernel,
        out_shape=(jax.ShapeDtypeStruct((M_SAMPLE, CONV_DIM), BF16),
                   jax.ShapeDtypeStruct((CONV_WIDTH - 1, DEC_BATCH, CONV_DIM), F32)),
        grid=(nc,),
        in_specs=[col(GATE_B_BLK), col(GATE_C_BLK), col(CONV_H_BLK),
                  pl.BlockSpec((CONV_WIDTH, CONV_COLS), lambda c: (0, c)),
                  pl.BlockSpec((CONV_WIDTH - 1, DEC_BATCH, CONV_COLS), lambda c: (0, 0, c))],
        out_specs=(pl.BlockSpec((M_SAMPLE, CONV_COLS), lambda c: (0, c)),
                   pl.BlockSpec((CONV_WIDTH - 1, DEC_BATCH, CONV_COLS), lambda c: (0, 0, c))),
        compiler_params=_params(("parallel",)),
        name="conv_sample",
    )(proj, proj, proj, conv_w, state)


def _to_step_major(a):
    return a.transpose(1, 0, 2).reshape(M_SAMPLE, a.shape[-1])


def _to_seq_major(a):
    return a.reshape(DEC_SEQ, DEC_BATCH, a.shape[-1]).transpose(1, 0, 2)


def _ffn(x, mod, k0, g, w1, w3, w2):
    h = _norm_mod(x, g, mod, k0, k0 + 1)
    u = _mm_swiglu(h, w1.astype(BF16), w3.astype(BF16))
    return _mm_res(u, w2.astype(BF16), x, mod, k0 + 2, 0.5, tm=512, tn=512, name="mm_w2")


def kernel(x_prompt, x_sample, c_prompt, c_sample, cache_k, cache_v, state_conv, rel_bias,
           g_ffn1, w1_ffn1, w3_ffn1, w2_ffn1, g_mix, w_in, sinks, conv_w, w_out,
           g_ffn2, w1_ffn2, w3_ffn2, w2_ffn2, w_ada, b_ada, g_final):
    depth = w_in.shape[0]
    wb = cache_k.shape[2]
    bucket = jnp.asarray(_bucket_map())
    table = rel_bias.reshape(N_BUCKETS * N_HEADS)

    x = jnp.concatenate([x_prompt.reshape(M_PROMPT, D_MODEL), _to_step_major(x_sample)], axis=0)
    c_all = jnp.concatenate(
        [c_sample, jnp.broadcast_to(c_prompt, (ROW_GROUP, D_MODEL))], axis=0)

    kp, vp, cp, ks, vs, cs = [], [], [], [], [], []
    for l in range(depth):
        mod = _ada(c_all, w_ada[l], b_ada[l])
        x = _ffn(x, mod, 0, g_ffn1[l], w1_ffn1[l], w3_ffn1[l], w2_ffn1[l])

        h = _norm_mod(x, g_mix[l], mod, 3, 4)
        proj = _mm_plain(h, w_in[l].astype(BF16))

        o_p = _attn_prompt(proj, table, sinks[l], bucket)
        proj_s = proj[M_PROMPT:, :ATTN_DIM + 2 * KV_DIM]
        q_s = _to_seq_major(proj_s[:, :ATTN_DIM])
        kn_s = _to_seq_major(proj_s[:, ATTN_DIM:ATTN_DIM + KV_DIM])
        vn_s = _to_seq_major(proj_s[:, ATTN_DIM + KV_DIM:])
        o_s, kw_s, vw_s = _attn_sample(
            q_s, kn_s, vn_s, cache_k[l].reshape(DEC_BATCH, wb, KV_DIM),
            cache_v[l].reshape(DEC_BATCH, wb, KV_DIM), table, sinks[l], bucket)
        yc_p, tail_p = _conv_prompt(proj, conv_w[l])
        yc_s, tail_s = _conv_sample(proj, conv_w[l], state_conv[l].transpose(1, 0, 2))
        mixed = jnp.concatenate(
            [jnp.concatenate([o_p, _to_step_major(o_s)], axis=0),
             jnp.concatenate([yc_p, yc_s], axis=0)], axis=1)
        x = _mm_res(mixed, w_out[l].astype(BF16), x, mod, 5, 1.0, tm=1024, tn=512, name="mm_out")

        x = _ffn(x, mod, 6, g_ffn2[l], w1_ffn2[l], w3_ffn2[l], w2_ffn2[l])

        k_p = proj[M_PROMPT - wb:M_PROMPT, ATTN_DIM:ATTN_DIM + KV_DIM]
        v_p = proj[M_PROMPT - wb:M_PROMPT, ATTN_DIM + KV_DIM:ATTN_DIM + 2 * KV_DIM]
        kp.append(k_p.reshape(1, wb, N_KV_HEADS, HEAD_DIM))
        vp.append(v_p.reshape(1, wb, N_KV_HEADS, HEAD_DIM))
        cp.append(tail_p[TAIL_ROWS - (CONV_WIDTH - 1):].reshape(1, CONV_WIDTH - 1, CONV_DIM))
        ks.append(kw_s.reshape(DEC_BATCH, wb, N_KV_HEADS, HEAD_DIM))
        vs.append(vw_s.reshape(DEC_BATCH, wb, N_KV_HEADS, HEAD_DIM))
        cs.append(tail_s.transpose(1, 0, 2))

    y_prompt = _final_norm(x, g_final, 0, M_PROMPT).reshape(1, SEQ, D_MODEL)
    y_sample = _to_seq_major(_final_norm(x, g_final, M_PROMPT, M_SAMPLE))
    return (y_prompt, y_sample, jnp.stack(kp), jnp.stack(vp), jnp.stack(cp),
            jnp.stack(ks), jnp.stack(vs), jnp.stack(cs))
```

```python
import functools
import math

import numpy as np
import jax
import jax.numpy as jnp
from jax import lax
from jax.experimental import pallas as pl
from jax.experimental.pallas import tpu as pltpu

F32 = jnp.float32
BF16 = jnp.bfloat16

D_MODEL = 4096
SEQ = 8192
DEC_BATCH = 128
DEC_SEQ = 8
M_PROMPT = SEQ
M_SAMPLE = DEC_BATCH * DEC_SEQ
M_ALL = M_PROMPT + M_SAMPLE
HEAD_DIM = 128
ATTN_DIM = D_MODEL // 2
N_HEADS = ATTN_DIM // HEAD_DIM
N_KV_HEADS = 4
GQA_GROUP = N_HEADS // N_KV_HEADS
KV_DIM = N_KV_HEADS * HEAD_DIM
QKV_DIM = ATTN_DIM + 2 * KV_DIM
CONV_DIM = D_MODEL - ATTN_DIM
CONV_WIDTH = 3
WINDOW = 128
BLOCK = 128
N_BUCKETS = 32
MAX_DISTANCE = 128
D_FF = 11008
N_MOD = 9
PROJ_DIM = QKV_DIM + 3 * CONV_DIM
EPS = 1e-6
NEG = -1e30
SCALE = HEAD_DIM ** -0.5

ROW_GROUP = 128
SUBLANES = 8
VMEM_LIMIT_BYTES = 56 * 1024 * 1024


def _params(semantics):
    return pltpu.CompilerParams(dimension_semantics=semantics,
                                vmem_limit_bytes=VMEM_LIMIT_BYTES)


def _slab_index(row_block, rows_per_block):
    return (row_block < M_PROMPT // rows_per_block).astype(jnp.int32)


def _part_specs(parts, tr, cols, col_index, grid_rank):
    specs = []
    for _, row0, rows in parts:
        b0, nb = row0 // tr, rows // tr
        if grid_rank == 1:
            imap = lambda i, b0=b0, nb=nb: (jnp.clip(i - b0, 0, nb - 1), col_index())
        else:
            imap = lambda i, j, b0=b0, nb=nb: (jnp.clip(i - b0, 0, nb - 1), col_index(j))
        specs.append(pl.BlockSpec((tr, cols), imap))
    return specs


def _pick_part(refs, parts, tr, row_block):
    v = refs[0][...]
    for ref, (_, row0, _) in zip(refs[1:], parts[1:]):
        v = jnp.where(row_block >= row0 // tr, ref[...], v)
    return v


def _bucket_map():
    r = np.arange(BLOCK)[:, None]
    c = np.arange(2 * BLOCK)[None, :]
    n = np.maximum(BLOCK + r - c, 0)
    max_exact = N_BUCKETS // 2
    nf = np.maximum(n, 1).astype(np.float32)
    large = max_exact + (np.log(nf / max_exact) / math.log(MAX_DISTANCE / max_exact)
                         * (N_BUCKETS - max_exact)).astype(np.int32)
    large = np.minimum(large, N_BUCKETS - 1)
    return np.where(n < max_exact, n, large).astype(np.int32)


def _ada_kernel(c_ref, w_ref, b_ref, o_ref):
    c = c_ref[...]
    a = (c / (1.0 + jnp.exp(-c))).astype(BF16)
    o_ref[...] = jnp.dot(a, w_ref[...].astype(BF16),
                         preferred_element_type=F32) + b_ref[...]


def _ada(c_all, w_ada, b_ada, tn=1024):
    rows = c_all.shape[0]
    n = w_ada.shape[1]
    return pl.pallas_call(
        _ada_kernel,
        out_shape=jax.ShapeDtypeStruct((rows, n), F32),
        grid=(n // tn,),
        in_specs=[pl.BlockSpec((rows, D_MODEL), lambda j: (0, 0)),
                  pl.BlockSpec((D_MODEL, tn), lambda j: (0, j)),
                  pl.BlockSpec((1, tn), lambda j: (0, j))],
        out_specs=pl.BlockSpec((rows, tn), lambda j: (0, j)),
        compiler_params=_params(("parallel",)),
        name="ada",
    )(c_all, w_ada, b_ada.reshape(1, n))


def _norm_mod_kernel(*refs, parts, tr):
    n = len(parts)
    g_ref, sh_ref, sc_ref, o_ref = refs[n:]
    x = _pick_part(refs[:n], parts, tr, pl.program_id(0))
    d = x.shape[1]
    y = x * lax.rsqrt(jnp.mean(x * x, axis=-1, keepdims=True) + EPS) * g_ref[...]
    y = y.reshape(tr // ROW_GROUP, ROW_GROUP, d)
    o = y * (1.0 + sc_ref[...])[None] + sh_ref[...][None]
    o_ref[...] = o.reshape(tr, d).astype(BF16)


def _norm_mod(parts, g, mod, k_shift, k_scale):
    tr = 512 // len(parts)
    slab = lambda k: pl.BlockSpec((ROW_GROUP, D_MODEL), lambda i: (_slab_index(i, tr), k))
    return pl.pallas_call(
        functools.partial(_norm_mod_kernel, parts=parts, tr=tr),
        out_shape=jax.ShapeDtypeStruct((M_ALL, D_MODEL), BF16),
        grid=(M_ALL // tr,),
        in_specs=_part_specs(parts, tr, D_MODEL, lambda: 0, 1) + [
            pl.BlockSpec((1, D_MODEL), lambda i: (0, 0)), slab(k_shift), slab(k_scale)],
        out_specs=pl.BlockSpec((tr, D_MODEL), lambda i: (i, 0)),
        compiler_params=_params(("parallel",)),
        name="norm_mod",
    )(*[p[0] for p in parts], g.reshape(1, D_MODEL), mod, mod)


def _final_norm_kernel(x_ref, g_ref, o_ref):
    x = x_ref[...]
    o_ref[...] = x * lax.rsqrt(jnp.mean(x * x, axis=-1, keepdims=True) + EPS) * g_ref[...]


def _final_norm(x, g, row0, rows, tr=512):
    return pl.pallas_call(
        _final_norm_kernel,
        out_shape=jax.ShapeDtypeStruct((rows, D_MODEL), F32),
        grid=(rows // tr,),
        in_specs=[pl.BlockSpec((tr, D_MODEL), lambda i: (i + row0 // tr, 0)),
                  pl.BlockSpec((1, D_MODEL), lambda i: (0, 0))],
        out_specs=pl.BlockSpec((tr, D_MODEL), lambda i: (i, 0)),
        compiler_params=_params(("parallel",)),
        name="final_norm",
    )(x, g.reshape(1, D_MODEL))


def _resident_rows(tm, k, index_map):
    return pl.BlockSpec((tm, k), index_map, pipeline_mode=pl.Buffered(1))


def _mm_plain_kernel(x_ref, w_ref, o_ref):
    o_ref[...] = jnp.dot(x_ref[...], w_ref[...].astype(BF16), preferred_element_type=F32)


def _mm_plain(x, w, n, tm=2304, tn=512):
    m, k = x.shape
    return pl.pallas_call(
        _mm_plain_kernel,
        out_shape=jax.ShapeDtypeStruct((m, n), F32),
        grid=(m // tm, n // tn),
        in_specs=[_resident_rows(tm, k, lambda i, j: (i, 0)),
                  pl.BlockSpec((k, tn), lambda i, j: (0, j))],
        out_specs=pl.BlockSpec((tm, tn), lambda i, j: (i, j)),
        compiler_params=_params(("parallel", "parallel")),
        name="mm_qkv",
    )(x, w)


def _mm_swiglu_kernel(x_ref, w1_ref, w3_ref, w2_ref, o_ref, w2_bf16_ref):
    x = x_ref[...]
    a = jnp.dot(x, w1_ref[...].astype(BF16), preferred_element_type=F32)
    b = jnp.dot(x, w3_ref[...].astype(BF16), preferred_element_type=F32)
    o_ref[...] = (a / (1.0 + jnp.exp(-a)) * b).astype(BF16)

    @pl.when(pl.program_id(0) == 0)
    def _():
        w2_bf16_ref[...] = w2_ref[...].astype(BF16)


def _mm_swiglu(x, w1, w3, w2, tm=2304, tn=256):
    m, k = x.shape
    n = w1.shape[1]
    nj = n // tn
    w2_rows = w2.shape[0] // nj
    w2_blk = lambda i, j: (jnp.where(i == 0, j, nj - 1), 0)
    return pl.pallas_call(
        _mm_swiglu_kernel,
        out_shape=(jax.ShapeDtypeStruct((m, n), BF16),
                   jax.ShapeDtypeStruct(w2.shape, BF16)),
        grid=(m // tm, nj),
        in_specs=[_resident_rows(tm, k, lambda i, j: (i, 0)),
                  pl.BlockSpec((k, tn), lambda i, j: (0, j)),
                  pl.BlockSpec((k, tn), lambda i, j: (0, j)),
                  pl.BlockSpec((w2_rows, w2.shape[1]), w2_blk)],
        out_specs=(pl.BlockSpec((tm, tn), lambda i, j: (i, j)),
                   pl.BlockSpec((w2_rows, w2.shape[1]), w2_blk)),
        compiler_params=_params(("arbitrary", "arbitrary")),
        name="mm_swiglu",
    )(x, w1, w3, w2)


def _mm_res_kernel(*refs, x_parts, res_parts, tm, coef):
    i = pl.program_id(0)
    pos = 0
    xs = []
    for parts in x_parts:
        xs.append(_pick_part(refs[pos:pos + len(parts)], parts, tm, i))
        pos += len(parts)
    w_ref = refs[pos]
    res = _pick_part(refs[pos + 1:pos + 1 + len(res_parts)], res_parts, tm, i)
    gate_ref, o_ref = refs[-2:]
    x = xs[0] if len(xs) == 1 else jnp.concatenate(xs, axis=1)
    acc = jnp.dot(x, w_ref[...].astype(BF16), preferred_element_type=F32)
    tn = acc.shape[1]
    gate = coef * gate_ref[...]
    y = acc.reshape(tm // ROW_GROUP, ROW_GROUP, tn) * gate[None]
    o_ref[...] = res + y.reshape(tm, tn)


def _mm_res(x_parts, w, res_parts, mod, k_gate, coef, tm, tn, name):
    k, n = w.shape
    gate_blocks = D_MODEL // tn
    in_specs, args = [], []
    for parts in x_parts:
        in_specs += _part_specs(parts, tm, parts[0][0].shape[1], lambda j: 0, 2)
        args += [p[0] for p in parts]
    in_specs.append(pl.BlockSpec((k, tn), lambda i, j: (0, j)))
    in_specs += _part_specs(res_parts, tm, tn, lambda j: j, 2)
    in_specs.append(pl.BlockSpec(
        (ROW_GROUP, tn), lambda i, j: (_slab_index(i, tm), k_gate * gate_blocks + j)))
    args += [w] + [p[0] for p in res_parts] + [mod]
    return pl.pallas_call(
        functools.partial(_mm_res_kernel, x_parts=x_parts, res_parts=res_parts, tm=tm,
                          coef=coef),
        out_shape=jax.ShapeDtypeStruct((M_ALL, n), F32),
        grid=(M_ALL // tm, n // tn),
        in_specs=in_specs,
        out_specs=pl.BlockSpec((tm, tn), lambda i, j: (i, j)),
        compiler_params=_params(("parallel", "parallel")),
        name=name,
    )(*args)


CONV_COLS = 256
GATE_B_BLK = QKV_DIM // CONV_COLS
GATE_C_BLK = GATE_B_BLK + CONV_DIM // CONV_COLS
CONV_H_BLK = GATE_C_BLK + CONV_DIM // CONV_COLS
TAIL_ROWS = (CONV_WIDTH - 1) * ROW_GROUP


def _mm_conv_kernel(x_ref, wb_ref, wc_ref, wh_ref, cw_ref, st_ref, yc_ref, tail_ref,
                    shift_scr, carry_scr):
    i = pl.program_id(0)
    c = pl.program_id(1)
    tm = x_ref.shape[0]
    x = x_ref[...]
    dot = lambda w_ref: jnp.dot(x, w_ref[...].astype(BF16), preferred_element_type=F32)
    gate_b = dot(wb_ref)
    u = dot(wc_ref) * dot(wh_ref)
    w0, w1, w2 = cw_ref[0:1, :], cw_ref[1:2, :], cw_ref[2:3, :]
    tail_ref[0] = u[tm - TAIL_ROWS:, :]
    n_prompt_blocks = M_PROMPT // tm

    @pl.when(i < n_prompt_blocks)
    def _():
        @pl.when(i == 0)
        def _():
            shift_scr[0:SUBLANES, :] = jnp.zeros((SUBLANES, CONV_COLS), F32)

        @pl.when(i > 0)
        def _():
            shift_scr[0:SUBLANES, :] = carry_scr[c]

        shift_scr[SUBLANES:SUBLANES + tm, :] = u
        carry_scr[c] = u[tm - SUBLANES:, :]
        u1 = shift_scr[SUBLANES - 1:SUBLANES - 1 + tm, :]
        u2 = shift_scr[SUBLANES - 2:SUBLANES - 2 + tm, :]
        yc_ref[...] = (gate_b * (u2 * w0 + u1 * w1 + u * w2)).astype(BF16)

    @pl.when(i >= n_prompt_blocks)
    def _():
        shape = (DEC_SEQ, DEC_BATCH, CONV_COLS)
        u3 = u.reshape(shape)
        st = st_ref[...]
        u1 = jnp.concatenate([st[1:2], u3[:DEC_SEQ - 1]], axis=0)
        u2 = jnp.concatenate([st[0:2], u3[:DEC_SEQ - 2]], axis=0)
        y = u2 * w0[None] + u1 * w1[None] + u3 * w2[None]
        yc_ref[...] = (gate_b.reshape(shape) * y).reshape(tm, CONV_COLS).astype(BF16)


def _mm_conv(x, w_in, conv_w, state, tm=M_SAMPLE):
    m, k = x.shape
    nc = CONV_DIM // CONV_COLS
    wcol = lambda base: pl.BlockSpec((k, CONV_COLS), lambda i, c: (0, base + c))
    return pl.pallas_call(
        _mm_conv_kernel,
        out_shape=(jax.ShapeDtypeStruct((m, CONV_DIM), BF16),
                   jax.ShapeDtypeStruct((m // tm, TAIL_ROWS, CONV_DIM), F32)),
        grid=(m // tm, nc),
        in_specs=[pl.BlockSpec((tm, k), lambda i, c: (i, 0)),
                  wcol(GATE_B_BLK), wcol(GATE_C_BLK), wcol(CONV_H_BLK),
                  pl.BlockSpec((CONV_WIDTH, CONV_COLS), lambda i, c: (0, c)),
                  pl.BlockSpec((CONV_WIDTH - 1, DEC_BATCH, CONV_COLS), lambda i, c: (0, 0, c))],
        out_specs=(pl.BlockSpec((tm, CONV_COLS), lambda i, c: (i, c)),
                   pl.BlockSpec((1, TAIL_ROWS, CONV_COLS), lambda i, c: (i, 0, c))),
        scratch_shapes=[pltpu.VMEM((tm + SUBLANES, CONV_COLS), F32),
                        pltpu.VMEM((nc, SUBLANES, CONV_COLS), F32)],
        compiler_params=_params(("arbitrary", "arbitrary")),
        name="mm_conv",
    )(x, w_in, w_in, w_in, conv_w, state)


def _build_bias(tab_ref, bucket, bias_scr):
    for h in range(N_HEADS):
        acc = jnp.zeros(bucket.shape, F32)
        for b in range(N_BUCKETS):
            acc = jnp.where(bucket == b, tab_ref[b * N_HEADS + h], acc)
        bias_scr[h] = acc


def _valid_mask(rows, has_prev):
    r = lax.broadcasted_iota(jnp.int32, (rows, 2 * BLOCK), 0)
    c = lax.broadcasted_iota(jnp.int32, (rows, 2 * BLOCK), 1)
    dist = BLOCK + r - c
    return (dist >= 0) & (dist <= WINDOW) & ((c >= BLOCK) | has_prev)


def _softmax_with_sink(s, sink):
    m = jnp.maximum(jnp.max(s, axis=-1, keepdims=True), sink)
    p = jnp.exp(s - m)
    denom = jnp.sum(p, axis=-1, keepdims=True) + jnp.exp(sink - m)
    return p / denom


def _attn_prompt_kernel(tab_ref, sink_ref, bucket_ref, q_ref, kc_ref, kp_ref, vc_ref, vp_ref,
                        o_ref, bias_scr):
    i = pl.program_id(0)

    @pl.when(i == 0)
    def _():
        _build_bias(tab_ref, bucket_ref[...], bias_scr)

    valid = _valid_mask(BLOCK, i > 0)
    for g in range(N_KV_HEADS):
        kv = slice(g * HEAD_DIM, (g + 1) * HEAD_DIM)
        heads = [g * GQA_GROUP + j for j in range(GQA_GROUP)]
        qs = jnp.concatenate(
            [q_ref[:, h * HEAD_DIM:(h + 1) * HEAD_DIM] for h in heads], axis=0).astype(BF16)
        kk = jnp.concatenate([kp_ref[:, kv], kc_ref[:, kv]], axis=0).astype(BF16)
        vv = jnp.concatenate([vp_ref[:, kv], vc_ref[:, kv]], axis=0).astype(BF16)
        s = lax.dot_general(qs, kk, (((1,), (1,)), ((), ())), preferred_element_type=F32)
        ps = []
        for j, h in enumerate(heads):
            sj = s[j * BLOCK:(j + 1) * BLOCK] * SCALE + bias_scr[h]
            sj = jnp.where(valid, sj, NEG)
            ps.append(_softmax_with_sink(sj, sink_ref[h]))
        p = jnp.concatenate(ps, axis=0).astype(BF16)
        o = jnp.dot(p, vv, preferred_element_type=F32)
        for j, h in enumerate(heads):
            o_ref[:, h * HEAD_DIM:(h + 1) * HEAD_DIM] = o[j * BLOCK:(j + 1) * BLOCK].astype(BF16)


def _attn_prompt(qkv, table, sinks, bucket):
    nb = M_PROMPT // BLOCK
    q_blk = 0
    k_blk = ATTN_DIM // KV_DIM
    v_blk = k_blk + 1
    prev = lambda i: jnp.maximum(i - 1, 0)
    smem = pl.BlockSpec(memory_space=pltpu.SMEM)
    return pl.pallas_call(
        _attn_prompt_kernel,
        out_shape=jax.ShapeDtypeStruct((M_PROMPT, ATTN_DIM), BF16),
        grid=(nb,),
        in_specs=[smem, smem,
                  pl.BlockSpec((BLOCK, 2 * BLOCK), lambda i: (0, 0)),
                  pl.BlockSpec((BLOCK, ATTN_DIM), lambda i: (i, q_blk)),
                  pl.BlockSpec((BLOCK, KV_DIM), lambda i: (i, k_blk)),
                  pl.BlockSpec((BLOCK, KV_DIM), lambda i: (prev(i), k_blk)),
                  pl.BlockSpec((BLOCK, KV_DIM), lambda i: (i, v_blk)),
                  pl.BlockSpec((BLOCK, KV_DIM), lambda i: (prev(i), v_blk))],
        out_specs=pl.BlockSpec((BLOCK, ATTN_DIM), lambda i: (i, 0)),
        scratch_shapes=[pltpu.VMEM((N_HEADS, BLOCK, 2 * BLOCK), F32)],
        compiler_params=_params(("arbitrary",)),
        name="attn_prompt",
    )(table, sinks, bucket, qkv, qkv, qkv, qkv, qkv)


def _attn_sample_kernel(tab_ref, sink_ref, bucket_ref, q_ref, kn_ref, vn_ref, ck_ref, cv_ref,
                        o_ref, kw_ref, vw_ref, bias_scr):
    i = pl.program_id(0)
    g_seq = q_ref.shape[0]
    wb = ck_ref.shape[1] // N_KV_HEADS

    @pl.when(i == 0)
    def _():
        _build_bias(tab_ref, bucket_ref[0:DEC_SEQ, :], bias_scr)

    valid = _valid_mask(DEC_SEQ, True)[None]
    pad = jnp.zeros((g_seq, 2 * BLOCK - wb - DEC_SEQ, HEAD_DIM), F32)
    for g in range(N_KV_HEADS):
        kv = slice(g * HEAD_DIM, (g + 1) * HEAD_DIM)
        head_rows = pl.ds(g, wb, stride=N_KV_HEADS)
        heads = [g * GQA_GROUP + j for j in range(GQA_GROUP)]
        qs = jnp.concatenate(
            [q_ref[:, :, h * HEAD_DIM:(h + 1) * HEAD_DIM] for h in heads], axis=1).astype(BF16)
        kk = jnp.concatenate([ck_ref[:, head_rows, :], kn_ref[:, :, kv], pad], axis=1).astype(BF16)
        vv = jnp.concatenate([cv_ref[:, head_rows, :], vn_ref[:, :, kv], pad], axis=1).astype(BF16)
        s = jnp.einsum('gqd,gkd->gqk', qs, kk, preferred_element_type=F32)
        ps = []
        for j, h in enumerate(heads):
            sj = s[:, j * DEC_SEQ:(j + 1) * DEC_SEQ] * SCALE + bias_scr[h][None]
            sj = jnp.where(valid, sj, NEG)
            ps.append(_softmax_with_sink(sj, sink_ref[h]))
        p = jnp.concatenate(ps, axis=1).astype(BF16)
        o = jnp.einsum('gqk,gkd->gqd', p, vv, preferred_element_type=F32)
        for j, h in enumerate(heads):
            o_ref[:, :, h * HEAD_DIM:(h + 1) * HEAD_DIM] = (
                o[:, j * DEC_SEQ:(j + 1) * DEC_SEQ].astype(BF16))
        new_rows = pl.ds((wb - DEC_SEQ) * N_KV_HEADS + g, DEC_SEQ, stride=N_KV_HEADS)
        kw_ref[:, new_rows, :] = kn_ref[:, :, kv]
        vw_ref[:, new_rows, :] = vn_ref[:, :, kv]
    kept = (wb - DEC_SEQ) * N_KV_HEADS
    kw_ref[:, 0:kept, :] = ck_ref[:, DEC_SEQ * N_KV_HEADS:, :]
    vw_ref[:, 0:kept, :] = cv_ref[:, DEC_SEQ * N_KV_HEADS:, :]


def _attn_sample(q, kn, vn, ck, cv, table, sinks, bucket, g_seq=8):
    rows = ck.shape[1]
    smem = pl.BlockSpec(memory_space=pltpu.SMEM)
    blk = lambda r, c: pl.BlockSpec((g_seq, r, c), lambda i: (i, 0, 0))
    return pl.pallas_call(
        _attn_sample_kernel,
        out_shape=(jax.ShapeDtypeStruct((DEC_BATCH, DEC_SEQ, ATTN_DIM), BF16),
                   jax.ShapeDtypeStruct((DEC_BATCH, rows, HEAD_DIM), F32),
                   jax.ShapeDtypeStruct((DEC_BATCH, rows, HEAD_DIM), F32)),
        grid=(DEC_BATCH // g_seq,),
        in_specs=[smem, smem,
                  pl.BlockSpec((BLOCK, 2 * BLOCK), lambda i: (0, 0)),
                  blk(DEC_SEQ, ATTN_DIM), blk(DEC_SEQ, KV_DIM), blk(DEC_SEQ, KV_DIM),
                  blk(rows, HEAD_DIM), blk(rows, HEAD_DIM)],
        out_specs=(blk(DEC_SEQ, ATTN_DIM), blk(rows, HEAD_DIM), blk(rows, HEAD_DIM)),
        scratch_shapes=[pltpu.VMEM((N_HEADS, DEC_SEQ, 2 * BLOCK), F32)],
        compiler_params=_params(("arbitrary",)),
        name="attn_sample",
    )(table, sinks, bucket, q, kn, vn, ck, cv)


def _to_step_major(a):
    return a.transpose(1, 0, 2).reshape(M_SAMPLE, a.shape[-1])


def _to_seq_major(a):
    return a.reshape(DEC_SEQ, DEC_BATCH, a.shape[-1]).transpose(1, 0, 2)


def _whole(a):
    return [(a, 0, M_ALL)]


def _ffn(x_parts, mod, k0, g, w1, w3, w2):
    h = _norm_mod(x_parts, g, mod, k0, k0 + 1)
    u, w2_bf16 = _mm_swiglu(h, w1, w3, w2)
    return _mm_res([_whole(u)], w2_bf16, x_parts, mod, k0 + 2, 0.5, tm=512, tn=512,
                   name="mm_w2")


def kernel(x_prompt, x_sample, c_prompt, c_sample, cache_k, cache_v, state_conv, rel_bias,
           g_ffn1, w1_ffn1, w3_ffn1, w2_ffn1, g_mix, w_in, sinks, conv_w, w_out,
           g_ffn2, w1_ffn2, w3_ffn2, w2_ffn2, w_ada, b_ada, g_final):
    depth = w_in.shape[0]
    wb = cache_k.shape[2]
    bucket = jnp.asarray(_bucket_map())
    table = rel_bias.reshape(N_BUCKETS * N_HEADS)
    c_all = jnp.concatenate(
        [c_sample, jnp.broadcast_to(c_prompt, (ROW_GROUP, D_MODEL))], axis=0)

    x_parts = [(x_prompt.reshape(M_PROMPT, D_MODEL), 0, M_PROMPT),
               (_to_step_major(x_sample), M_PROMPT, M_SAMPLE)]

    kp, vp, cp, ks, vs, cs = [], [], [], [], [], []
    for l in range(depth):
        mod = _ada(c_all, w_ada[l], b_ada[l])
        x = _ffn(x_parts, mod, 0, g_ffn1[l], w1_ffn1[l], w3_ffn1[l], w2_ffn1[l])

        h = _norm_mod(_whole(x), g_mix[l], mod, 3, 4)
        qkv = _mm_plain(h, w_in[l], QKV_DIM)
        yc, tails = _mm_conv(h, w_in[l], conv_w[l], state_conv[l].transpose(1, 0, 2))

        o_p = _attn_prompt(qkv, table, sinks[l], bucket)
        qkv_s = qkv[M_PROMPT:]
        q_s = _to_seq_major(qkv_s[:, :ATTN_DIM])
        kn_s = _to_seq_major(qkv_s[:, ATTN_DIM:ATTN_DIM + KV_DIM])
        vn_s = _to_seq_major(qkv_s[:, ATTN_DIM + KV_DIM:])
        o_s, kw_s, vw_s = _attn_sample(
            q_s, kn_s, vn_s,
            cache_k[l].reshape(DEC_BATCH, wb * N_KV_HEADS, HEAD_DIM),
            cache_v[l].reshape(DEC_BATCH, wb * N_KV_HEADS, HEAD_DIM), table, sinks[l], bucket)
        o_parts = [(o_p, 0, M_PROMPT), (_to_step_major(o_s), M_PROMPT, M_SAMPLE)]
        x = _mm_res([o_parts, _whole(yc)], w_out[l], _whole(x), mod, 5, 1.0, tm=1024, tn=512,
                    name="mm_out")

        x = _ffn(_whole(x), mod, 6, g_ffn2[l], w1_ffn2[l], w3_ffn2[l], w2_ffn2[l])
        x_parts = _whole(x)

        k_p = qkv[M_PROMPT - wb:M_PROMPT, ATTN_DIM:ATTN_DIM + KV_DIM]
        v_p = qkv[M_PROMPT - wb:M_PROMPT, ATTN_DIM + KV_DIM:]
        kp.append(k_p.reshape(1, wb, N_KV_HEADS, HEAD_DIM))
        vp.append(v_p.reshape(1, wb, N_KV_HEADS, HEAD_DIM))
        last_prompt = M_PROMPT // M_SAMPLE - 1
        cp.append(tails[last_prompt, TAIL_ROWS - (CONV_WIDTH - 1):].reshape(
            1, CONV_WIDTH - 1, CONV_DIM))
        ks.append(kw_s.reshape(DEC_BATCH, wb, N_KV_HEADS, HEAD_DIM))
        vs.append(vw_s.reshape(DEC_BATCH, wb, N_KV_HEADS, HEAD_DIM))
        cs.append(tails[last_prompt + 1].reshape(
            CONV_WIDTH - 1, DEC_BATCH, CONV_DIM).transpose(1, 0, 2))

    y_prompt = _final_norm(x, g_final, 0, M_PROMPT).reshape(1, SEQ, D_MODEL)
    y_sample = _to_seq_major(_final_norm(x, g_final, M_PROMPT, M_SAMPLE))
    return (y_prompt, y_sample, jnp.stack(kp), jnp.stack(vp), jnp.stack(cp),
            jnp.stack(ks), jnp.stack(vs), jnp.stack(cs))
```

```python
import functools
import math

import numpy as np
import jax
import jax.numpy as jnp
from jax import lax
from jax.experimental import pallas as pl
from jax.experimental.pallas import tpu as pltpu

F32 = jnp.float32
BF16 = jnp.bfloat16

D_MODEL = 4096
SEQ = 8192
DEC_BATCH = 128
DEC_SEQ = 8
M_PROMPT = SEQ
M_SAMPLE = DEC_BATCH * DEC_SEQ
M_ALL = M_PROMPT + M_SAMPLE
HEAD_DIM = 128
ATTN_DIM = D_MODEL // 2
N_HEADS = ATTN_DIM // HEAD_DIM
N_KV_HEADS = 4
GQA_GROUP = N_HEADS // N_KV_HEADS
KV_DIM = N_KV_HEADS * HEAD_DIM
QKV_DIM = ATTN_DIM + 2 * KV_DIM
CONV_DIM = D_MODEL - ATTN_DIM
CONV_WIDTH = 3
WINDOW = 128
BLOCK = 128
N_BUCKETS = 32
MAX_DISTANCE = 128
D_FF = 11008
N_MOD = 9
PROJ_DIM = QKV_DIM + 3 * CONV_DIM
EPS = 1e-6
NEG = -1e30
SCALE = HEAD_DIM ** -0.5

ROW_GROUP = 128
SUBLANES = 8
VMEM_LIMIT_BYTES = 56 * 1024 * 1024


def _params(semantics):
    return pltpu.CompilerParams(dimension_semantics=semantics,
                                vmem_limit_bytes=VMEM_LIMIT_BYTES)


def _slab_index(row_block, rows_per_block):
    return (row_block < M_PROMPT // rows_per_block).astype(jnp.int32)


def _part_specs(parts, tr, cols, col_index, grid_rank):
    specs = []
    for _, row0, rows in parts:
        b0, nb = row0 // tr, rows // tr
        if grid_rank == 1:
            imap = lambda i, b0=b0, nb=nb: (jnp.clip(i - b0, 0, nb - 1), col_index())
        else:
            imap = lambda i, j, b0=b0, nb=nb: (jnp.clip(i - b0, 0, nb - 1), col_index(j))
        specs.append(pl.BlockSpec((tr, cols), imap))
    return specs


def _pick_part(refs, parts, tr, row_block):
    v = refs[0][...]
    for ref, (_, row0, _) in zip(refs[1:], parts[1:]):
        v = jnp.where(row_block >= row0 // tr, ref[...], v)
    return v


def _bucket_map():
    r = np.arange(BLOCK)[:, None]
    c = np.arange(2 * BLOCK)[None, :]
    n = np.maximum(BLOCK + r - c, 0)
    max_exact = N_BUCKETS // 2
    nf = np.maximum(n, 1).astype(np.float32)
    large = max_exact + (np.log(nf / max_exact) / math.log(MAX_DISTANCE / max_exact)
                         * (N_BUCKETS - max_exact)).astype(np.int32)
    large = np.minimum(large, N_BUCKETS - 1)
    return np.where(n < max_exact, n, large).astype(np.int32)


def _ada_kernel(c_ref, w_ref, b_ref, o_ref):
    c = c_ref[...]
    a = (c / (1.0 + jnp.exp(-c))).astype(BF16)
    o_ref[...] = jnp.dot(a, w_ref[...].astype(BF16),
                         preferred_element_type=F32) + b_ref[...]


def _ada(c_all, w_ada, b_ada, tn=1024):
    rows = c_all.shape[0]
    n = w_ada.shape[1]
    return pl.pallas_call(
        _ada_kernel,
        out_shape=jax.ShapeDtypeStruct((rows, n), F32),
        grid=(n // tn,),
        in_specs=[pl.BlockSpec((rows, D_MODEL), lambda j: (0, 0)),
                  pl.BlockSpec((D_MODEL, tn), lambda j: (0, j)),
                  pl.BlockSpec((1, tn), lambda j: (0, j))],
        out_specs=pl.BlockSpec((rows, tn), lambda j: (0, j)),
        compiler_params=_params(("parallel",)),
        name="ada",
    )(c_all, w_ada, b_ada.reshape(1, n))


def _norm_mod_kernel(*refs, parts, tr):
    n = len(parts)
    g_ref, sh_ref, sc_ref, o_ref = refs[n:]
    x = _pick_part(refs[:n], parts, tr, pl.program_id(0))
    d = x.shape[1]
    y = x * lax.rsqrt(jnp.mean(x * x, axis=-1, keepdims=True) + EPS) * g_ref[...]
    y = y.reshape(tr // ROW_GROUP, ROW_GROUP, d)
    o = y * (1.0 + sc_ref[...])[None] + sh_ref[...][None]
    o_ref[...] = o.reshape(tr, d).astype(BF16)


def _norm_mod(parts, g, mod, k_shift, k_scale):
    tr = 512 // len(parts)
    slab = lambda k: pl.BlockSpec((ROW_GROUP, D_MODEL), lambda i: (_slab_index(i, tr), k))
    return pl.pallas_call(
        functools.partial(_norm_mod_kernel, parts=parts, tr=tr),
        out_shape=jax.ShapeDtypeStruct((M_ALL, D_MODEL), BF16),
        grid=(M_ALL // tr,),
        in_specs=_part_specs(parts, tr, D_MODEL, lambda: 0, 1) + [
            pl.BlockSpec((1, D_MODEL), lambda i: (0, 0)), slab(k_shift), slab(k_scale)],
        out_specs=pl.BlockSpec((tr, D_MODEL), lambda i: (i, 0)),
        compiler_params=_params(("parallel",)),
        name="norm_mod",
    )(*[p[0] for p in parts], g.reshape(1, D_MODEL), mod, mod)


def _final_norm_kernel(x_ref, g_ref, o_ref):
    x = x_ref[...]
    o_ref[...] = x * lax.rsqrt(jnp.mean(x * x, axis=-1, keepdims=True) + EPS) * g_ref[...]


def _final_norm(x, g, row0, rows, tr=512):
    return pl.pallas_call(
        _final_norm_kernel,
        out_shape=jax.ShapeDtypeStruct((rows, D_MODEL), F32),
        grid=(rows // tr,),
        in_specs=[pl.BlockSpec((tr, D_MODEL), lambda i: (i + row0 // tr, 0)),
                  pl.BlockSpec((1, D_MODEL), lambda i: (0, 0))],
        out_specs=pl.BlockSpec((tr, D_MODEL), lambda i: (i, 0)),
        compiler_params=_params(("parallel",)),
        name="final_norm",
    )(x, g.reshape(1, D_MODEL))


def _resident_rows(tm, k, index_map):
    return pl.BlockSpec((tm, k), index_map, pipeline_mode=pl.Buffered(1))


def _mm_plain_kernel(x_ref, w_ref, o_ref):
    o_ref[...] = jnp.dot(x_ref[...], w_ref[...].astype(BF16), preferred_element_type=F32)


def _mm_plain(x, w, n, tm=1536, tn=512):
    m, k = x.shape
    return pl.pallas_call(
        _mm_plain_kernel,
        out_shape=jax.ShapeDtypeStruct((m, n), F32),
        grid=(m // tm, n // tn),
        in_specs=[pl.BlockSpec((tm, k), lambda i, j: (i, 0)),
                  pl.BlockSpec((k, tn), lambda i, j: (0, j))],
        out_specs=pl.BlockSpec((tm, tn), lambda i, j: (i, j)),
        compiler_params=_params(("parallel", "parallel")),
        name="mm_qkv",
    )(x, w)


W2_SWEEPS = 2


def _mm_swiglu_kernel(x_ref, w1_ref, w3_ref, w2_ref, o_ref, w2_bf16_ref, *, w2_steps):
    x = x_ref[...]
    a = jnp.dot(x, w1_ref[...].astype(BF16), preferred_element_type=F32)
    b = jnp.dot(x, w3_ref[...].astype(BF16), preferred_element_type=F32)
    o_ref[...] = (a / (1.0 + jnp.exp(-a)) * b).astype(BF16)

    @pl.when(pl.program_id(0) * pl.num_programs(1) + pl.program_id(1) < w2_steps)
    def _():
        cols = w2_bf16_ref.shape[2]
        for nb in range(w2_bf16_ref.shape[0]):
            w2_bf16_ref[nb] = w2_ref[:, nb * cols:(nb + 1) * cols].astype(BF16)


W2_COLS = 512


def _mm_swiglu(x, w1, w3, w2, tm=3072, tn=256):
    m, k = x.shape
    n = w1.shape[1]
    nj = n // tn
    w2_steps = W2_SWEEPS * nj
    assert m // tm >= W2_SWEEPS and w2.shape[0] % w2_steps == 0
    w2_rows = w2.shape[0] // w2_steps
    w2_step = lambda i, j: jnp.minimum(i * nj + j, w2_steps - 1)
    w2_nb = w2.shape[1] // W2_COLS
    return pl.pallas_call(
        functools.partial(_mm_swiglu_kernel, w2_steps=w2_steps),
        out_shape=(jax.ShapeDtypeStruct((m, n), BF16),
                   jax.ShapeDtypeStruct((w2_nb, w2.shape[0], W2_COLS), BF16)),
        grid=(m // tm, nj),
        in_specs=[_resident_rows(tm, k, lambda i, j: (i, 0)),
                  pl.BlockSpec((k, tn), lambda i, j: (0, j)),
                  pl.BlockSpec((k, tn), lambda i, j: (0, j)),
                  pl.BlockSpec((w2_rows, w2.shape[1]), lambda i, j: (w2_step(i, j), 0))],
        out_specs=(pl.BlockSpec((tm, tn), lambda i, j: (i, j)),
                   pl.BlockSpec((w2_nb, w2_rows, W2_COLS),
                                lambda i, j: (0, w2_step(i, j), 0))),
        compiler_params=_params(("arbitrary", "arbitrary")),
        name="mm_swiglu",
    )(x, w1, w3, w2)


def _mm_res_kernel(*refs, x_parts, res_parts, tm, coef):
    i = pl.program_id(0)
    pos = 0
    xs = []
    for parts in x_parts:
        xs.append(_pick_part(refs[pos:pos + len(parts)], parts, tm, i))
        pos += len(parts)
    w_ref = refs[pos]
    res = _pick_part(refs[pos + 1:pos + 1 + len(res_parts)], res_parts, tm, i)
    gate_ref, o_ref = refs[-2:]
    x = xs[0] if len(xs) == 1 else jnp.concatenate(xs, axis=1)
    acc = jnp.dot(x, w_ref[...].astype(BF16), preferred_element_type=F32)
    tn = acc.shape[1]
    gate = coef * gate_ref[...]
    y = acc.reshape(tm // ROW_GROUP, ROW_GROUP, tn) * gate[None]
    o_ref[...] = res + y.reshape(tm, tn)


def _mm_res(x_parts, w, res_parts, mod, k_gate, coef, tm, tn, name):
    if w.ndim == 3:
        assert w.shape[2] == tn
        k, n = w.shape[1], w.shape[0] * tn
        w_spec = pl.BlockSpec((None, k, tn), lambda i, j: (j, 0, 0))
    else:
        k, n = w.shape
        w_spec = pl.BlockSpec((k, tn), lambda i, j: (0, j))
    gate_blocks = D_MODEL // tn
    in_specs, args = [], []
    for parts in x_parts:
        in_specs += _part_specs(parts, tm, parts[0][0].shape[1], lambda j: 0, 2)
        args += [p[0] for p in parts]
    in_specs.append(w_spec)
    in_specs += _part_specs(res_parts, tm, tn, lambda j: j, 2)
    in_specs.append(pl.BlockSpec(
        (ROW_GROUP, tn), lambda i, j: (_slab_index(i, tm), k_gate * gate_blocks + j)))
    args += [w] + [p[0] for p in res_parts] + [mod]
    return pl.pallas_call(
        functools.partial(_mm_res_kernel, x_parts=x_parts, res_parts=res_parts, tm=tm,
                          coef=coef),
        out_shape=jax.ShapeDtypeStruct((M_ALL, n), F32),
        grid=(M_ALL // tm, n // tn),
        in_specs=in_specs,
        out_specs=pl.BlockSpec((tm, tn), lambda i, j: (i, j)),
        compiler_params=_params(("parallel", "parallel")),
        name=name,
    )(*args)


CONV_COLS = 256
GATE_B_BLK = QKV_DIM // CONV_COLS
GATE_C_BLK = GATE_B_BLK + CONV_DIM // CONV_COLS
CONV_H_BLK = GATE_C_BLK + CONV_DIM // CONV_COLS
TAIL_ROWS = (CONV_WIDTH - 1) * ROW_GROUP


def _mm_conv_kernel(x_ref, wb_ref, wc_ref, wh_ref, cw_ref, st_ref, yc_ref, tail_ref,
                    shift_scr, carry_scr):
    i = pl.program_id(0)
    c = pl.program_id(1)
    tm = x_ref.shape[0]
    x = x_ref[...]
    dot = lambda w_ref: jnp.dot(x, w_ref[...].astype(BF16), preferred_element_type=F32)
    gate_b = dot(wb_ref)
    u = dot(wc_ref) * dot(wh_ref)
    w0, w1, w2 = cw_ref[0:1, :], cw_ref[1:2, :], cw_ref[2:3, :]
    tail_ref[0] = u[tm - TAIL_ROWS:, :]
    n_prompt_blocks = M_PROMPT // tm

    @pl.when(i < n_prompt_blocks)
    def _():
        @pl.when(i == 0)
        def _():
            shift_scr[0:SUBLANES, :] = jnp.zeros((SUBLANES, CONV_COLS), F32)

        @pl.when(i > 0)
        def _():
            shift_scr[0:SUBLANES, :] = carry_scr[c]

        shift_scr[SUBLANES:SUBLANES + tm, :] = u
        carry_scr[c] = u[tm - SUBLANES:, :]
        u1 = shift_scr[SUBLANES - 1:SUBLANES - 1 + tm, :]
        u2 = shift_scr[SUBLANES - 2:SUBLANES - 2 + tm, :]
        yc_ref[...] = (gate_b * (u2 * w0 + u1 * w1 + u * w2)).astype(BF16)

    @pl.when(i >= n_prompt_blocks)
    def _():
        shape = (DEC_SEQ, DEC_BATCH, CONV_COLS)
        u3 = u.reshape(shape)
        st = st_ref[...]
        u1 = jnp.concatenate([st[1:2], u3[:DEC_SEQ - 1]], axis=0)
        u2 = jnp.concatenate([st[0:2], u3[:DEC_SEQ - 2]], axis=0)
        y = u2 * w0[None] + u1 * w1[None] + u3 * w2[None]
        yc_ref[...] = (gate_b.reshape(shape) * y).reshape(tm, CONV_COLS).astype(BF16)


def _mm_conv(x, w_in, conv_w, state, tm=M_SAMPLE):
    m, k = x.shape
    nc = CONV_DIM // CONV_COLS
    wcol = lambda base: pl.BlockSpec((k, CONV_COLS), lambda i, c: (0, base + c))
    return pl.pallas_call(
        _mm_conv_kernel,
        out_shape=(jax.ShapeDtypeStruct((m, CONV_DIM), BF16),
                   jax.ShapeDtypeStruct((m // tm, TAIL_ROWS, CONV_DIM), F32)),
        grid=(m // tm, nc),
        in_specs=[pl.BlockSpec((tm, k), lambda i, c: (i, 0)),
                  wcol(GATE_B_BLK), wcol(GATE_C_BLK), wcol(CONV_H_BLK),
                  pl.BlockSpec((CONV_WIDTH, CONV_COLS), lambda i, c: (0, c)),
                  pl.BlockSpec((CONV_WIDTH - 1, DEC_BATCH, CONV_COLS), lambda i, c: (0, 0, c))],
        out_specs=(pl.BlockSpec((tm, CONV_COLS), lambda i, c: (i, c)),
                   pl.BlockSpec((1, TAIL_ROWS, CONV_COLS), lambda i, c: (i, 0, c))),
        scratch_shapes=[pltpu.VMEM((tm + SUBLANES, CONV_COLS), F32),
                        pltpu.VMEM((nc, SUBLANES, CONV_COLS), F32)],
        compiler_params=_params(("arbitrary", "arbitrary")),
        name="mm_conv",
    )(x, w_in, w_in, w_in, conv_w, state)


def _build_bias(tab_ref, bucket, bias_scr):
    for h in range(N_HEADS):
        acc = jnp.zeros(bucket.shape, F32)
        for b in range(N_BUCKETS):
            acc = jnp.where(bucket == b, tab_ref[b * N_HEADS + h], acc)
        bias_scr[h] = acc


def _valid_mask(rows, has_prev):
    r = lax.broadcasted_iota(jnp.int32, (rows, 2 * BLOCK), 0)
    c = lax.broadcasted_iota(jnp.int32, (rows, 2 * BLOCK), 1)
    dist = BLOCK + r - c
    return (dist >= 0) & (dist <= WINDOW) & ((c >= BLOCK) | has_prev)


def _softmax_with_sink(s, sink):
    m = jnp.maximum(jnp.max(s, axis=-1, keepdims=True), sink)
    p = jnp.exp(s - m)
    denom = jnp.sum(p, axis=-1, keepdims=True) + jnp.exp(sink - m)
    return p / denom


def _attn_prompt_kernel(tab_ref, sink_ref, bucket_ref, q_ref, kc_ref, kp_ref, vc_ref, vp_ref,
                        o_ref, bias_scr):
    i = pl.program_id(0)

    @pl.when(i == 0)
    def _():
        _build_bias(tab_ref, bucket_ref[...], bias_scr)

    valid = _valid_mask(BLOCK, i > 0)
    for g in range(N_KV_HEADS):
        kv = slice(g * HEAD_DIM, (g + 1) * HEAD_DIM)
        heads = [g * GQA_GROUP + j for j in range(GQA_GROUP)]
        qs = jnp.concatenate(
            [q_ref[:, h * HEAD_DIM:(h + 1) * HEAD_DIM] for h in heads], axis=0).astype(BF16)
        kk = jnp.concatenate([kp_ref[:, kv], kc_ref[:, kv]], axis=0).astype(BF16)
        vv = jnp.concatenate([vp_ref[:, kv], vc_ref[:, kv]], axis=0).astype(BF16)
        s = lax.dot_general(qs, kk, (((1,), (1,)), ((), ())), preferred_element_type=F32)
        ps = []
        for j, h in enumerate(heads):
            sj = s[j * BLOCK:(j + 1) * BLOCK] * SCALE + bias_scr[h]
            sj = jnp.where(valid, sj, NEG)
            ps.append(_softmax_with_sink(sj, sink_ref[h]))
        p = jnp.concatenate(ps, axis=0).astype(BF16)
        o = jnp.dot(p, vv, preferred_element_type=F32)
        for j, h in enumerate(heads):
            o_ref[:, h * HEAD_DIM:(h + 1) * HEAD_DIM] = o[j * BLOCK:(j + 1) * BLOCK].astype(BF16)


def _attn_prompt(qkv, table, sinks, bucket):
    nb = M_PROMPT // BLOCK
    q_blk = 0
    k_blk = ATTN_DIM // KV_DIM
    v_blk = k_blk + 1
    prev = lambda i: jnp.maximum(i - 1, 0)
    smem = pl.BlockSpec(memory_space=pltpu.SMEM)
    return pl.pallas_call(
        _attn_prompt_kernel,
        out_shape=jax.ShapeDtypeStruct((M_PROMPT, ATTN_DIM), BF16),
        grid=(nb,),
        in_specs=[smem, smem,
                  pl.BlockSpec((BLOCK, 2 * BLOCK), lambda i: (0, 0)),
                  pl.BlockSpec((BLOCK, ATTN_DIM), lambda i: (i, q_blk)),
                  pl.BlockSpec((BLOCK, KV_DIM), lambda i: (i, k_blk)),
                  pl.BlockSpec((BLOCK, KV_DIM), lambda i: (prev(i), k_blk)),
                  pl.BlockSpec((BLOCK, KV_DIM), lambda i: (i, v_blk)),
                  pl.BlockSpec((BLOCK, KV_DIM), lambda i: (prev(i), v_blk))],
        out_specs=pl.BlockSpec((BLOCK, ATTN_DIM), lambda i: (i, 0)),
        scratch_shapes=[pltpu.VMEM((N_HEADS, BLOCK, 2 * BLOCK), F32)],
        compiler_params=_params(("arbitrary",)),
        name="attn_prompt",
    )(table, sinks, bucket, qkv, qkv, qkv, qkv, qkv)


def _attn_sample_kernel(tab_ref, sink_ref, bucket_ref, q_ref, kn_ref, vn_ref, ck_ref, cv_ref,
                        o_ref, kw_ref, vw_ref, bias_scr):
    i = pl.program_id(0)
    g_seq = q_ref.shape[0]
    wb = ck_ref.shape[1] // N_KV_HEADS

    @pl.when(i == 0)
    def _():
        _build_bias(tab_ref, bucket_ref[0:DEC_SEQ, :], bias_scr)

    valid = _valid_mask(DEC_SEQ, True)[None]
    pad = jnp.zeros((g_seq, 2 * BLOCK - wb - DEC_SEQ, HEAD_DIM), F32)
    for g in range(N_KV_HEADS):
        kv = slice(g * HEAD_DIM, (g + 1) * HEAD_DIM)
        head_rows = pl.ds(g, wb, stride=N_KV_HEADS)
        heads = [g * GQA_GROUP + j for j in range(GQA_GROUP)]
        qs = jnp.concatenate(
            [q_ref[:, :, h * HEAD_DIM:(h + 1) * HEAD_DIM] for h in heads], axis=1).astype(BF16)
        kk = jnp.concatenate([ck_ref[:, head_rows, :], kn_ref[:, :, kv], pad], axis=1).astype(BF16)
        vv = jnp.concatenate([cv_ref[:, head_rows, :], vn_ref[:, :, kv], pad], axis=1).astype(BF16)
        s = jnp.einsum('gqd,gkd->gqk', qs, kk, preferred_element_type=F32)
        ps = []
        for j, h in enumerate(heads):
            sj = s[:, j * DEC_SEQ:(j + 1) * DEC_SEQ] * SCALE + bias_scr[h][None]
            sj = jnp.where(valid, sj, NEG)
            ps.append(_softmax_with_sink(sj, sink_ref[h]))
        p = jnp.concatenate(ps, axis=1).astype(BF16)
        o = jnp.einsum('gqk,gkd->gqd', p, vv, preferred_element_type=F32)
        for j, h in enumerate(heads):
            o_ref[:, :, h * HEAD_DIM:(h + 1) * HEAD_DIM] = (
                o[:, j * DEC_SEQ:(j + 1) * DEC_SEQ].astype(BF16))
        new_rows = pl.ds((wb - DEC_SEQ) * N_KV_HEADS + g, DEC_SEQ, stride=N_KV_HEADS)
        kw_ref[:, new_rows, :] = kn_ref[:, :, kv]
        vw_ref[:, new_rows, :] = vn_ref[:, :, kv]
    kept = (wb - DEC_SEQ) * N_KV_HEADS
    kw_ref[:, 0:kept, :] = ck_ref[:, DEC_SEQ * N_KV_HEADS:, :]
    vw_ref[:, 0:kept, :] = cv_ref[:, DEC_SEQ * N_KV_HEADS:, :]


def _attn_sample(q, kn, vn, ck, cv, table, sinks, bucket, g_seq=8):
    rows = ck.shape[1]
    smem = pl.BlockSpec(memory_space=pltpu.SMEM)
    blk = lambda r, c: pl.BlockSpec((g_seq, r, c), lambda i: (i, 0, 0))
    return pl.pallas_call(
        _attn_sample_kernel,
        out_shape=(jax.ShapeDtypeStruct((DEC_BATCH, DEC_SEQ, ATTN_DIM), BF16),
                   jax.ShapeDtypeStruct((DEC_BATCH, rows, HEAD_DIM), F32),
                   jax.ShapeDtypeStruct((DEC_BATCH, rows, HEAD_DIM), F32)),
        grid=(DEC_BATCH // g_seq,),
        in_specs=[smem, smem,
                  pl.BlockSpec((BLOCK, 2 * BLOCK), lambda i: (0, 0)),
                  blk(DEC_SEQ, ATTN_DIM), blk(DEC_SEQ, KV_DIM), blk(DEC_SEQ, KV_DIM),
                  blk(rows, HEAD_DIM), blk(rows, HEAD_DIM)],
        out_specs=(blk(DEC_SEQ, ATTN_DIM), blk(rows, HEAD_DIM), blk(rows, HEAD_DIM)),
        scratch_shapes=[pltpu.VMEM((N_HEADS, DEC_SEQ, 2 * BLOCK), F32)],
        compiler_params=_params(("arbitrary",)),
        name="attn_sample",
    )(table, sinks, bucket, q, kn, vn, ck, cv)


def _to_step_major(a):
    return a.transpose(1, 0, 2).reshape(M_SAMPLE, a.shape[-1])


def _to_seq_major(a):
    return a.reshape(DEC_SEQ, DEC_BATCH, a.shape[-1]).transpose(1, 0, 2)


def _whole(a):
    return [(a, 0, M_ALL)]


def _ffn(x_parts, mod, k0, g, w1, w3, w2):
    h = _norm_mod(x_parts, g, mod, k0, k0 + 1)
    u, w2_bf16 = _mm_swiglu(h, w1, w3, w2)
    return _mm_res([_whole(u)], w2_bf16, x_parts, mod, k0 + 2, 0.5, tm=512, tn=512,
                   name="mm_w2")


def kernel(x_prompt, x_sample, c_prompt, c_sample, cache_k, cache_v, state_conv, rel_bias,
           g_ffn1, w1_ffn1, w3_ffn1, w2_ffn1, g_mix, w_in, sinks, conv_w, w_out,
           g_ffn2, w1_ffn2, w3_ffn2, w2_ffn2, w_ada, b_ada, g_final):
    depth = w_in.shape[0]
    wb = cache_k.shape[2]
    bucket = jnp.asarray(_bucket_map())
    table = rel_bias.reshape(N_BUCKETS * N_HEADS)
    c_all = jnp.concatenate(
        [c_sample, jnp.broadcast_to(c_prompt, (ROW_GROUP, D_MODEL))], axis=0)

    x_parts = [(x_prompt.reshape(M_PROMPT, D_MODEL), 0, M_PROMPT),
               (_to_step_major(x_sample), M_PROMPT, M_SAMPLE)]

    kp, vp, cp, ks, vs, cs = [], [], [], [], [], []
    for l in range(depth):
        mod = _ada(c_all, w_ada[l], b_ada[l])
        x = _ffn(x_parts, mod, 0, g_ffn1[l], w1_ffn1[l], w3_ffn1[l], w2_ffn1[l])

        h = _norm_mod(_whole(x), g_mix[l], mod, 3, 4)
        qkv = _mm_plain(h, w_in[l], QKV_DIM)
        yc, tails = _mm_conv(h, w_in[l], conv_w[l], state_conv[l].transpose(1, 0, 2))

        o_p = _attn_prompt(qkv, table, sinks[l], bucket)
        qkv_s = qkv[M_PROMPT:]
        q_s = _to_seq_major(qkv_s[:, :ATTN_DIM])
        kn_s = _to_seq_major(qkv_s[:, ATTN_DIM:ATTN_DIM + KV_DIM])
        vn_s = _to_seq_major(qkv_s[:, ATTN_DIM + KV_DIM:])
        o_s, kw_s, vw_s = _attn_sample(
            q_s, kn_s, vn_s,
            cache_k[l].reshape(DEC_BATCH, wb * N_KV_HEADS, HEAD_DIM),
            cache_v[l].reshape(DEC_BATCH, wb * N_KV_HEADS, HEAD_DIM), table, sinks[l], bucket)
        o_parts = [(o_p, 0, M_PROMPT), (_to_step_major(o_s), M_PROMPT, M_SAMPLE)]
        x = _mm_res([o_parts, _whole(yc)], w_out[l], _whole(x), mod, 5, 1.0, tm=1024, tn=512,
                    name="mm_out")

        x = _ffn(_whole(x), mod, 6, g_ffn2[l], w1_ffn2[l], w3_ffn2[l], w2_ffn2[l])
        x_parts = _whole(x)

        k_p = qkv[M_PROMPT - wb:M_PROMPT, ATTN_DIM:ATTN_DIM + KV_DIM]
        v_p = qkv[M_PROMPT - wb:M_PROMPT, ATTN_DIM + KV_DIM:]
        kp.append(k_p.reshape(1, wb, N_KV_HEADS, HEAD_DIM))
        vp.append(v_p.reshape(1, wb, N_KV_HEADS, HEAD_DIM))
        last_prompt = M_PROMPT // M_SAMPLE - 1
        cp.append(tails[last_prompt, TAIL_ROWS - (CONV_WIDTH - 1):].reshape(
            1, CONV_WIDTH - 1, CONV_DIM))
        ks.append(kw_s.reshape(DEC_BATCH, wb, N_KV_HEADS, HEAD_DIM))
        vs.append(vw_s.reshape(DEC_BATCH, wb, N_KV_HEADS, HEAD_DIM))
        cs.append(tails[last_prompt + 1].reshape(
            CONV_WIDTH - 1, DEC_BATCH, CONV_DIM).transpose(1, 0, 2))

    y_prompt = _final_norm(x, g_final, 0, M_PROMPT).reshape(1, SEQ, D_MODEL)
    y_sample = _to_seq_major(_final_norm(x, g_final, M_PROMPT, M_SAMPLE))
    return (y_prompt, y_sample, jnp.stack(kp), jnp.stack(vp), jnp.stack(cp),
            jnp.stack(ks), jnp.stack(vs), jnp.stack(cs))
```

```python
import functools
import math

import numpy as np
import jax
import jax.numpy as jnp
from jax import lax
from jax.experimental import pallas as pl
from jax.experimental.pallas import tpu as pltpu

F32 = jnp.float32
BF16 = jnp.bfloat16

D_MODEL = 4096
SEQ = 8192
DEC_BATCH = 128
DEC_SEQ = 8
M_PROMPT = SEQ
M_SAMPLE = DEC_BATCH * DEC_SEQ
M_ALL = M_PROMPT + M_SAMPLE
HEAD_DIM = 128
ATTN_DIM = D_MODEL // 2
N_HEADS = ATTN_DIM // HEAD_DIM
N_KV_HEADS = 4
GQA_GROUP = N_HEADS // N_KV_HEADS
KV_DIM = N_KV_HEADS * HEAD_DIM
QKV_DIM = ATTN_DIM + 2 * KV_DIM
CONV_DIM = D_MODEL - ATTN_DIM
CONV_WIDTH = 3
WINDOW = 128
BLOCK = 128
N_BUCKETS = 32
MAX_DISTANCE = 128
D_FF = 11008
N_MOD = 9
PROJ_DIM = QKV_DIM + 3 * CONV_DIM
EPS = 1e-6
NEG = -1e30
SCALE = HEAD_DIM ** -0.5

ROW_GROUP = 128
SUBLANES = 8
VMEM_LIMIT_BYTES = 56 * 1024 * 1024


def _params(semantics):
    return pltpu.CompilerParams(dimension_semantics=semantics,
                                vmem_limit_bytes=VMEM_LIMIT_BYTES)


def _slab_index(row_block, rows_per_block):
    return (row_block < M_PROMPT // rows_per_block).astype(jnp.int32)


def _part_specs(parts, tr, cols, col_index, grid_rank):
    specs = []
    for _, row0, rows in parts:
        b0, nb = row0 // tr, rows // tr
        if grid_rank == 1:
            imap = lambda i, b0=b0, nb=nb: (jnp.clip(i - b0, 0, nb - 1), col_index())
        else:
            imap = lambda i, j, b0=b0, nb=nb: (jnp.clip(i - b0, 0, nb - 1), col_index(j))
        specs.append(pl.BlockSpec((tr, cols), imap))
    return specs


def _pick_part(refs, parts, tr, row_block):
    v = refs[0][...]
    for ref, (_, row0, _) in zip(refs[1:], parts[1:]):
        v = jnp.where(row_block >= row0 // tr, ref[...], v)
    return v


def _bucket_map():
    r = np.arange(BLOCK)[:, None]
    c = np.arange(2 * BLOCK)[None, :]
    n = np.maximum(BLOCK + r - c, 0)
    max_exact = N_BUCKETS // 2
    nf = np.maximum(n, 1).astype(np.float32)
    large = max_exact + (np.log(nf / max_exact) / math.log(MAX_DISTANCE / max_exact)
                         * (N_BUCKETS - max_exact)).astype(np.int32)
    large = np.minimum(large, N_BUCKETS - 1)
    return np.where(n < max_exact, n, large).astype(np.int32)


ADA_ROWS = ROW_GROUP + 16


def _silu_kernel(c_ref, o_ref):
    c = c_ref[...]
    o_ref[...] = (c / (1.0 + jnp.exp(-c))).astype(BF16)


def _silu_bf16(c):
    return pl.pallas_call(
        _silu_kernel, out_shape=jax.ShapeDtypeStruct(c.shape, BF16), name="silu")(c)


def _ada_block(a_ref, w_ref, b_ref):
    r = jnp.dot(a_ref[...], w_ref[...].astype(BF16), preferred_element_type=F32) + b_ref[...]
    prompt = jnp.broadcast_to(r[ROW_GROUP:ROW_GROUP + 1], (ROW_GROUP, r.shape[1]))
    return jnp.concatenate([r[:ROW_GROUP], prompt], axis=0)


def _ada_kernel(a_ref, w_ref, b_ref, o_ref):
    o_ref[...] = _ada_block(a_ref, w_ref, b_ref)


def _ada(a, w_ada, b_ada, n, tn=1024):
    return pl.pallas_call(
        _ada_kernel,
        out_shape=jax.ShapeDtypeStruct((2 * ROW_GROUP, n), F32),
        grid=(n // tn,),
        in_specs=[pl.BlockSpec((ADA_ROWS, D_MODEL), lambda j: (0, 0)),
                  pl.BlockSpec((D_MODEL, tn), lambda j: (0, j)),
                  pl.BlockSpec((1, tn), lambda j: (0, j))],
        out_specs=pl.BlockSpec((2 * ROW_GROUP, tn), lambda j: (0, j)),
        compiler_params=_params(("parallel",)),
        name="ada",
    )(a, w_ada, b_ada)


def _norm_mod_kernel(*refs, parts, tr):
    n = len(parts)
    g_ref, sh_ref, sc_ref, o_ref = refs[n:]
    x = _pick_part(refs[:n], parts, tr, pl.program_id(0))
    d = x.shape[1]
    y = x * lax.rsqrt(jnp.mean(x * x, axis=-1, keepdims=True) + EPS) * g_ref[...]
    y = y.reshape(tr // ROW_GROUP, ROW_GROUP, d)
    o = y * (1.0 + sc_ref[...])[None] + sh_ref[...][None]
    o_ref[...] = o.reshape(tr, d).astype(BF16)


def _norm_mod(parts, g, shift, scale):
    tr = 512 // len(parts)
    slab = lambda v: pl.BlockSpec((ROW_GROUP, D_MODEL), lambda i: (_slab_index(i, tr), v[1]))
    return pl.pallas_call(
        functools.partial(_norm_mod_kernel, parts=parts, tr=tr),
        out_shape=jax.ShapeDtypeStruct((M_ALL, D_MODEL), BF16),
        grid=(M_ALL // tr,),
        in_specs=_part_specs(parts, tr, D_MODEL, lambda: 0, 1) + [
            pl.BlockSpec((1, D_MODEL), lambda i: (0, 0)), slab(shift), slab(scale)],
        out_specs=pl.BlockSpec((tr, D_MODEL), lambda i: (i, 0)),
        compiler_params=_params(("parallel",)),
        name="norm_mod",
    )(*[p[0] for p in parts], g.reshape(1, D_MODEL), shift[0], scale[0])


def _final_norm_kernel(x_ref, g_ref, o_ref):
    x = x_ref[...]
    o_ref[...] = x * lax.rsqrt(jnp.mean(x * x, axis=-1, keepdims=True) + EPS) * g_ref[...]


def _final_norm(x, g, row0, rows, tr=512):
    return pl.pallas_call(
        _final_norm_kernel,
        out_shape=jax.ShapeDtypeStruct((rows, D_MODEL), F32),
        grid=(rows // tr,),
        in_specs=[pl.BlockSpec((tr, D_MODEL), lambda i: (i + row0 // tr, 0)),
                  pl.BlockSpec((1, D_MODEL), lambda i: (0, 0))],
        out_specs=pl.BlockSpec((tr, D_MODEL), lambda i: (i, 0)),
        compiler_params=_params(("parallel",)),
        name="final_norm",
    )(x, g.reshape(1, D_MODEL))


def _resident_rows(tm, k, index_map):
    return pl.BlockSpec((tm, k), index_map, pipeline_mode=pl.Buffered(1))


def _mm_plain_kernel(x_ref, w_ref, o_ref):
    o_ref[...] = jnp.dot(x_ref[...], w_ref[...].astype(BF16), preferred_element_type=F32)


def _mm_plain(x, w, n, tm=1536, tn=512):
    m, k = x.shape
    return pl.pallas_call(
        _mm_plain_kernel,
        out_shape=jax.ShapeDtypeStruct((m, n), F32),
        grid=(m // tm, n // tn),
        in_specs=[pl.BlockSpec((tm, k), lambda i, j: (i, 0)),
                  pl.BlockSpec((k, tn), lambda i, j: (0, j))],
        out_specs=pl.BlockSpec((tm, tn), lambda i, j: (i, j)),
        compiler_params=_params(("parallel", "parallel")),
        name="mm_qkv",
    )(x, w)


def _mm_swiglu_kernel(*refs, w2_steps, ada_steps):
    x_ref, w1_ref, w3_ref, w2_ref = refs[:4]
    o_ref, w2_bf16_ref = refs[-3:-1] if ada_steps else refs[-2:]
    x = x_ref[...]
    a = jnp.dot(x, w1_ref[...].astype(BF16), preferred_element_type=F32)
    b = jnp.dot(x, w3_ref[...].astype(BF16), preferred_element_type=F32)
    o_ref[...] = (a / (1.0 + jnp.exp(-a)) * b).astype(BF16)
    step = pl.program_id(0) * pl.num_programs(1) + pl.program_id(1)

    @pl.when(step < w2_steps)
    def _():
        cols = w2_bf16_ref.shape[2]
        for nb in range(w2_bf16_ref.shape[0]):
            w2_bf16_ref[nb] = w2_ref[:, nb * cols:(nb + 1) * cols].astype(BF16)

    if ada_steps:
        @pl.when(step < ada_steps)
        def _():
            refs[-1][...] = _ada_block(*refs[4:7])


W2_COLS = 512
ADA_COLS = 256


def _mm_swiglu(x, w1, w3, w2, ada=None, tm=3072, tn=256, w2_sweeps=2):
    m, k = x.shape
    n = w1.shape[1]
    nj = n // tn
    steps = (m // tm) * nj
    step = lambda i, j: i * nj + j
    w2_steps = w2_sweeps * nj
    assert steps >= w2_steps and w2.shape[0] % w2_steps == 0
    w2_rows = w2.shape[0] // w2_steps
    w2_blk = lambda i, j: jnp.minimum(step(i, j), w2_steps - 1)
    w2_nb = w2.shape[1] // W2_COLS
    in_specs = [_resident_rows(tm, k, lambda i, j: (i, 0)),
                pl.BlockSpec((k, tn), lambda i, j: (0, j)),
                pl.BlockSpec((k, tn), lambda i, j: (0, j)),
                pl.BlockSpec((w2_rows, w2.shape[1]), lambda i, j: (w2_blk(i, j), 0))]
    out_shape = [jax.ShapeDtypeStruct((m, n), BF16),
                 jax.ShapeDtypeStruct((w2_nb, w2.shape[0], W2_COLS), BF16)]
    out_specs = [pl.BlockSpec((tm, tn), lambda i, j: (i, j)),
                 pl.BlockSpec((w2_nb, w2_rows, W2_COLS), lambda i, j: (0, w2_blk(i, j), 0))]
    args = [x, w1, w3, w2]
    ada_steps = 0
    if ada is not None:
        a, w_ada, b_ada, col0, n_ada = ada
        ada_steps = n_ada // ADA_COLS
        assert steps >= ada_steps
        ada_blk = lambda i, j: jnp.minimum(step(i, j), ada_steps - 1)
        in_specs += [
            pl.BlockSpec(a.shape, lambda i, j: (0, 0), pipeline_mode=pl.Buffered(1)),
            pl.BlockSpec((k, ADA_COLS), lambda i, j: (0, col0 // ADA_COLS + ada_blk(i, j))),
            pl.BlockSpec((1, ADA_COLS), lambda i, j: (0, col0 // ADA_COLS + ada_blk(i, j)))]
        out_shape.append(jax.ShapeDtypeStruct((2 * ROW_GROUP, n_ada), F32))
        out_specs.append(pl.BlockSpec((2 * ROW_GROUP, ADA_COLS), lambda i, j: (0, ada_blk(i, j))))
        args += [a, w_ada, b_ada]
    return pl.pallas_call(
        functools.partial(_mm_swiglu_kernel, w2_steps=w2_steps, ada_steps=ada_steps),
        out_shape=tuple(out_shape),
        grid=(m // tm, nj),
        in_specs=in_specs,
        out_specs=tuple(out_specs),
        compiler_params=_params(("arbitrary", "arbitrary")),
        name="mm_swiglu",
    )(*args)


def _mm_res_kernel(*refs, x_parts, res_parts, tm, coef):
    i = pl.program_id(0)
    pos = 0
    xs = []
    for parts in x_parts:
        xs.append(_pick_part(refs[pos:pos + len(parts)], parts, tm, i))
        pos += len(parts)
    w_ref = refs[pos]
    res = _pick_part(refs[pos + 1:pos + 1 + len(res_parts)], res_parts, tm, i)
    gate_ref, o_ref = refs[-2:]
    x = xs[0] if len(xs) == 1 else jnp.concatenate(xs, axis=1)
    acc = jnp.dot(x, w_ref[...].astype(BF16), preferred_element_type=F32)
    tn = acc.shape[1]
    gate = coef * gate_ref[...]
    y = acc.reshape(tm // ROW_GROUP, ROW_GROUP, tn) * gate[None]
    o_ref[...] = res + y.reshape(tm, tn)


def _mm_res(x_parts, w, res_parts, gate, coef, tm, tn, name):
    mod, k_gate = gate
    if w.ndim == 3:
        assert w.shape[2] == tn
        k, n = w.shape[1], w.shape[0] * tn
        w_spec = pl.BlockSpec((None, k, tn), lambda i, j: (j, 0, 0))
    else:
        k, n = w.shape
        w_spec = pl.BlockSpec((k, tn), lambda i, j: (0, j))
    gate_blocks = D_MODEL // tn
    in_specs, args = [], []
    for parts in x_parts:
        in_specs += _part_specs(parts, tm, parts[0][0].shape[1], lambda j: 0, 2)
        args += [p[0] for p in parts]
    in_specs.append(w_spec)
    in_specs += _part_specs(res_parts, tm, tn, lambda j: j, 2)
    in_specs.append(pl.BlockSpec(
        (ROW_GROUP, tn), lambda i, j: (_slab_index(i, tm), k_gate * gate_blocks + j)))
    args += [w] + [p[0] for p in res_parts] + [mod]
    return pl.pallas_call(
        functools.partial(_mm_res_kernel, x_parts=x_parts, res_parts=res_parts, tm=tm,
                          coef=coef),
        out_shape=jax.ShapeDtypeStruct((M_ALL, n), F32),
        grid=(M_ALL // tm, n // tn),
        in_specs=in_specs,
        out_specs=pl.BlockSpec((tm, tn), lambda i, j: (i, j)),
        compiler_params=_params(("parallel", "parallel")),
        name=name,
    )(*args)


CONV_COLS = 256
GATE_B_BLK = QKV_DIM // CONV_COLS
GATE_C_BLK = GATE_B_BLK + CONV_DIM // CONV_COLS
CONV_H_BLK = GATE_C_BLK + CONV_DIM // CONV_COLS
TAIL_ROWS = (CONV_WIDTH - 1) * ROW_GROUP


def _mm_conv_kernel(x_ref, wb_ref, wc_ref, wh_ref, cw_ref, st_ref, yc_ref, tail_ref,
                    shift_scr, carry_scr):
    i = pl.program_id(0)
    c = pl.program_id(1)
    tm = x_ref.shape[0]
    x = x_ref[...]
    dot = lambda w_ref: jnp.dot(x, w_ref[...].astype(BF16), preferred_element_type=F32)
    gate_b = dot(wb_ref)
    u = dot(wc_ref) * dot(wh_ref)
    w0, w1, w2 = cw_ref[0:1, :], cw_ref[1:2, :], cw_ref[2:3, :]
    tail_ref[0] = u[tm - TAIL_ROWS:, :]
    n_prompt_blocks = M_PROMPT // tm

    @pl.when(i < n_prompt_blocks)
    def _():
        @pl.when(i == 0)
        def _():
            shift_scr[0:SUBLANES, :] = jnp.zeros((SUBLANES, CONV_COLS), F32)

        @pl.when(i > 0)
        def _():
            shift_scr[0:SUBLANES, :] = carry_scr[c]

        shift_scr[SUBLANES:SUBLANES + tm, :] = u
        carry_scr[c] = u[tm - SUBLANES:, :]
        u1 = shift_scr[SUBLANES - 1:SUBLANES - 1 + tm, :]
        u2 = shift_scr[SUBLANES - 2:SUBLANES - 2 + tm, :]
        yc_ref[...] = (gate_b * (u2 * w0 + u1 * w1 + u * w2)).astype(BF16)

    @pl.when(i >= n_prompt_blocks)
    def _():
        shape = (DEC_SEQ, DEC_BATCH, CONV_COLS)
        u3 = u.reshape(shape)
        st = st_ref[...]
        u1 = jnp.concatenate([st[1:2], u3[:DEC_SEQ - 1]], axis=0)
        u2 = jnp.concatenate([st[0:2], u3[:DEC_SEQ - 2]], axis=0)
        y = u2 * w0[None] + u1 * w1[None] + u3 * w2[None]
        yc_ref[...] = (gate_b.reshape(shape) * y).reshape(tm, CONV_COLS).astype(BF16)


def _mm_conv(x, w_in, conv_w, state, tm=M_SAMPLE):
    m, k = x.shape
    nc = CONV_DIM // CONV_COLS
    wcol = lambda base: pl.BlockSpec((k, CONV_COLS), lambda i, c: (0, base + c))
    return pl.pallas_call(
        _mm_conv_kernel,
        out_shape=(jax.ShapeDtypeStruct((m, CONV_DIM), BF16),
                   jax.ShapeDtypeStruct((m // tm, TAIL_ROWS, CONV_DIM), F32)),
        grid=(m // tm, nc),
        in_specs=[pl.BlockSpec((tm, k), lambda i, c: (i, 0)),
                  wcol(GATE_B_BLK), wcol(GATE_C_BLK), wcol(CONV_H_BLK),
                  pl.BlockSpec((CONV_WIDTH, CONV_COLS), lambda i, c: (0, c)),
                  pl.BlockSpec((CONV_WIDTH - 1, DEC_BATCH, CONV_COLS), lambda i, c: (0, 0, c))],
        out_specs=(pl.BlockSpec((tm, CONV_COLS), lambda i, c: (i, c)),
                   pl.BlockSpec((1, TAIL_ROWS, CONV_COLS), lambda i, c: (i, 0, c))),
        scratch_shapes=[pltpu.VMEM((tm + SUBLANES, CONV_COLS), F32),
                        pltpu.VMEM((nc, SUBLANES, CONV_COLS), F32)],
        compiler_params=_params(("arbitrary", "arbitrary")),
        name="mm_conv",
    )(x, w_in, w_in, w_in, conv_w, state)


def _build_bias(tab_ref, bucket, bias_scr):
    for h in range(N_HEADS):
        acc = jnp.zeros(bucket.shape, F32)
        for b in range(N_BUCKETS):
            acc = jnp.where(bucket == b, tab_ref[b * N_HEADS + h], acc)
        bias_scr[h] = acc


def _valid_mask(rows, has_prev):
    r = lax.broadcasted_iota(jnp.int32, (rows, 2 * BLOCK), 0)
    c = lax.broadcasted_iota(jnp.int32, (rows, 2 * BLOCK), 1)
    dist = BLOCK + r - c
    return (dist >= 0) & (dist <= WINDOW) & ((c >= BLOCK) | has_prev)


def _softmax_with_sink(s, sink):
    m = jnp.maximum(jnp.max(s, axis=-1, keepdims=True), sink)
    p = jnp.exp(s - m)
    denom = jnp.sum(p, axis=-1, keepdims=True) + jnp.exp(sink - m)
    return p / denom


def _attn_prompt_kernel(tab_ref, sink_ref, bucket_ref, q_ref, kc_ref, kp_ref, vc_ref, vp_ref,
                        o_ref, bias_scr):
    i = pl.program_id(0)

    @pl.when(i == 0)
    def _():
        _build_bias(tab_ref, bucket_ref[...], bias_scr)

    valid = _valid_mask(BLOCK, i > 0)
    for g in range(N_KV_HEADS):
        kv = slice(g * HEAD_DIM, (g + 1) * HEAD_DIM)
        heads = [g * GQA_GROUP + j for j in range(GQA_GROUP)]
        qs = jnp.concatenate(
            [q_ref[:, h * HEAD_DIM:(h + 1) * HEAD_DIM] for h in heads], axis=0).astype(BF16)
        kk = jnp.concatenate([kp_ref[:, kv], kc_ref[:, kv]], axis=0).astype(BF16)
        vv = jnp.concatenate([vp_ref[:, kv], vc_ref[:, kv]], axis=0).astype(BF16)
        s = lax.dot_general(qs, kk, (((1,), (1,)), ((), ())), preferred_element_type=F32)
        ps = []
        for j, h in enumerate(heads):
            sj = s[j * BLOCK:(j + 1) * BLOCK] * SCALE + bias_scr[h]
            sj = jnp.where(valid, sj, NEG)
            ps.append(_softmax_with_sink(sj, sink_ref[h]))
        p = jnp.concatenate(ps, axis=0).astype(BF16)
        o = jnp.dot(p, vv, preferred_element_type=F32)
        for j, h in enumerate(heads):
            o_ref[:, h * HEAD_DIM:(h + 1) * HEAD_DIM] = o[j * BLOCK:(j + 1) * BLOCK].astype(BF16)


def _attn_prompt(qkv, table, sinks, bucket):
    nb = M_PROMPT // BLOCK
    q_blk = 0
    k_blk = ATTN_DIM // KV_DIM
    v_blk = k_blk + 1
    prev = lambda i: jnp.maximum(i - 1, 0)
    smem = pl.BlockSpec(memory_space=pltpu.SMEM)
    return pl.pallas_call(
        _attn_prompt_kernel,
        out_shape=jax.ShapeDtypeStruct((M_PROMPT, ATTN_DIM), BF16),
        grid=(nb,),
        in_specs=[smem, smem,
                  pl.BlockSpec((BLOCK, 2 * BLOCK), lambda i: (0, 0)),
                  pl.BlockSpec((BLOCK, ATTN_DIM), lambda i: (i, q_blk)),
                  pl.BlockSpec((BLOCK, KV_DIM), lambda i: (i, k_blk)),
                  pl.BlockSpec((BLOCK, KV_DIM), lambda i: (prev(i), k_blk)),
                  pl.BlockSpec((BLOCK, KV_DIM), lambda i: (i, v_blk)),
                  pl.BlockSpec((BLOCK, KV_DIM), lambda i: (prev(i), v_blk))],
        out_specs=pl.BlockSpec((BLOCK, ATTN_DIM), lambda i: (i, 0)),
        scratch_shapes=[pltpu.VMEM((N_HEADS, BLOCK, 2 * BLOCK), F32)],
        compiler_params=_params(("arbitrary",)),
        name="attn_prompt",
    )(table, sinks, bucket, qkv, qkv, qkv, qkv, qkv)


def _attn_sample_kernel(tab_ref, sink_ref, bucket_ref, q_ref, kn_ref, vn_ref, ck_ref, cv_ref,
                        o_ref, kw_ref, vw_ref, bias_scr):
    i = pl.program_id(0)
    g_seq = q_ref.shape[0]
    wb = ck_ref.shape[1] // N_KV_HEADS

    @pl.when(i == 0)
    def _():
        _build_bias(tab_ref, bucket_ref[0:DEC_SEQ, :], bias_scr)

    valid = _valid_mask(DEC_SEQ, True)[None]
    pad = jnp.zeros((g_seq, 2 * BLOCK - wb - DEC_SEQ, HEAD_DIM), F32)
    for g in range(N_KV_HEADS):
        kv = slice(g * HEAD_DIM, (g + 1) * HEAD_DIM)
        head_rows = pl.ds(g, wb, stride=N_KV_HEADS)
        heads = [g * GQA_GROUP + j for j in range(GQA_GROUP)]
        qs = jnp.concatenate(
            [q_ref[:, :, h * HEAD_DIM:(h + 1) * HEAD_DIM] for h in heads], axis=1).astype(BF16)
        kk = jnp.concatenate([ck_ref[:, head_rows, :], kn_ref[:, :, kv], pad], axis=1).astype(BF16)
        vv = jnp.concatenate([cv_ref[:, head_rows, :], vn_ref[:, :, kv], pad], axis=1).astype(BF16)
        s = jnp.einsum('gqd,gkd->gqk', qs, kk, preferred_element_type=F32)
        ps = []
        for j, h in enumerate(heads):
            sj = s[:, j * DEC_SEQ:(j + 1) * DEC_SEQ] * SCALE + bias_scr[h][None]
            sj = jnp.where(valid, sj, NEG)
            ps.append(_softmax_with_sink(sj, sink_ref[h]))
        p = jnp.concatenate(ps, axis=1).astype(BF16)
        o = jnp.einsum('gqk,gkd->gqd', p, vv, preferred_element_type=F32)
        for j, h in enumerate(heads):
            o_ref[:, :, h * HEAD_DIM:(h + 1) * HEAD_DIM] = (
                o[:, j * DEC_SEQ:(j + 1) * DEC_SEQ].astype(BF16))
        new_rows = pl.ds((wb - DEC_SEQ) * N_KV_HEADS + g, DEC_SEQ, stride=N_KV_HEADS)
        kw_ref[:, new_rows, :] = kn_ref[:, :, kv]
        vw_ref[:, new_rows, :] = vn_ref[:, :, kv]
    kept = (wb - DEC_SEQ) * N_KV_HEADS
    kw_ref[:, 0:kept, :] = ck_ref[:, DEC_SEQ * N_KV_HEADS:, :]
    vw_ref[:, 0:kept, :] = cv_ref[:, DEC_SEQ * N_KV_HEADS:, :]


def _attn_sample(q, kn, vn, ck, cv, table, sinks, bucket, g_seq=8):
    rows = ck.shape[1]
    smem = pl.BlockSpec(memory_space=pltpu.SMEM)
    blk = lambda r, c: pl.BlockSpec((g_seq, r, c), lambda i: (i, 0, 0))
    return pl.pallas_call(
        _attn_sample_kernel,
        out_shape=(jax.ShapeDtypeStruct((DEC_BATCH, DEC_SEQ, ATTN_DIM), BF16),
                   jax.ShapeDtypeStruct((DEC_BATCH, rows, HEAD_DIM), F32),
                   jax.ShapeDtypeStruct((DEC_BATCH, rows, HEAD_DIM), F32)),
        grid=(DEC_BATCH // g_seq,),
        in_specs=[smem, smem,
                  pl.BlockSpec((BLOCK, 2 * BLOCK), lambda i: (0, 0)),
                  blk(DEC_SEQ, ATTN_DIM), blk(DEC_SEQ, KV_DIM), blk(DEC_SEQ, KV_DIM),
                  blk(rows, HEAD_DIM), blk(rows, HEAD_DIM)],
        out_specs=(blk(DEC_SEQ, ATTN_DIM), blk(rows, HEAD_DIM), blk(rows, HEAD_DIM)),
        scratch_shapes=[pltpu.VMEM((N_HEADS, DEC_SEQ, 2 * BLOCK), F32)],
        compiler_params=_params(("arbitrary",)),
        name="attn_sample",
    )(table, sinks, bucket, q, kn, vn, ck, cv)


def _to_step_major(a):
    return a.transpose(1, 0, 2).reshape(M_SAMPLE, a.shape[-1])


def _to_seq_major(a):
    return a.reshape(DEC_SEQ, DEC_BATCH, a.shape[-1]).transpose(1, 0, 2)


def _whole(a):
    return [(a, 0, M_ALL)]


N_MOD_EARLY = 2


def _ffn(x_parts, mods, k0, g, w1, w3, w2, ada_rest=None):
    h = _norm_mod(x_parts, g, mods[k0], mods[k0 + 1])
    if ada_rest is None:
        u, w2_bf16 = _mm_swiglu(h, w1, w3, w2)
    else:
        n_rest = (N_MOD - N_MOD_EARLY) * D_MODEL
        u, w2_bf16, mod_rest = _mm_swiglu(
            h, w1, w3, w2, ada=ada_rest + (N_MOD_EARLY * D_MODEL, n_rest), tm=2304,
            w2_sweeps=4)
        mods += [(mod_rest, k) for k in range(N_MOD - N_MOD_EARLY)]
    return _mm_res([_whole(u)], w2_bf16, x_parts, mods[k0 + 2], 0.5, tm=512, tn=512,
                   name="mm_w2")


def kernel(x_prompt, x_sample, c_prompt, c_sample, cache_k, cache_v, state_conv, rel_bias,
           g_ffn1, w1_ffn1, w3_ffn1, w2_ffn1, g_mix, w_in, sinks, conv_w, w_out,
           g_ffn2, w1_ffn2, w3_ffn2, w2_ffn2, w_ada, b_ada, g_final):
    depth = w_in.shape[0]
    wb = cache_k.shape[2]
    bucket = jnp.asarray(_bucket_map())
    table = rel_bias.reshape(N_BUCKETS * N_HEADS)
    c_rows = jnp.concatenate(
        [c_sample, c_prompt, jnp.zeros((ADA_ROWS - ROW_GROUP - 1, D_MODEL), F32)], axis=0)
    silu_c = _silu_bf16(c_rows)

    x_parts = [(x_prompt.reshape(M_PROMPT, D_MODEL), 0, M_PROMPT),
               (_to_step_major(x_sample), M_PROMPT, M_SAMPLE)]

    kp, vp, cp, ks, vs, cs = [], [], [], [], [], []
    for l in range(depth):
        b_ada_l = b_ada[l].reshape(1, N_MOD * D_MODEL)
        mod_early = _ada(silu_c, w_ada[l], b_ada_l, N_MOD_EARLY * D_MODEL)
        mods = [(mod_early, k) for k in range(N_MOD_EARLY)]
        x = _ffn(x_parts, mods, 0, g_ffn1[l], w1_ffn1[l], w3_ffn1[l], w2_ffn1[l],
                 ada_rest=(silu_c, w_ada[l], b_ada_l))

        h = _norm_mod(_whole(x), g_mix[l], mods[3], mods[4])
        qkv = _mm_plain(h, w_in[l], QKV_DIM)
        yc, tails = _mm_conv(h, w_in[l], conv_w[l], state_conv[l].transpose(1, 0, 2))

        o_p = _attn_prompt(qkv, table, sinks[l], bucket)
        qkv_s = qkv[M_PROMPT:]
        q_s = _to_seq_major(qkv_s[:, :ATTN_DIM])
        kn_s = _to_seq_major(qkv_s[:, ATTN_DIM:ATTN_DIM + KV_DIM])
        vn_s = _to_seq_major(qkv_s[:, ATTN_DIM + KV_DIM:])
        o_s, kw_s, vw_s = _attn_sample(
            q_s, kn_s, vn_s,
            cache_k[l].reshape(DEC_BATCH, wb * N_KV_HEADS, HEAD_DIM),
            cache_v[l].reshape(DEC_BATCH, wb * N_KV_HEADS, HEAD_DIM), table, sinks[l], bucket)
        o_parts = [(o_p, 0, M_PROMPT), (_to_step_major(o_s), M_PROMPT, M_SAMPLE)]
        x = _mm_res([o_parts, _whole(yc)], w_out[l], _whole(x), mods[5], 1.0, tm=1024, tn=512,
                    name="mm_out")

        x = _ffn(_whole(x), mods, 6, g_ffn2[l], w1_ffn2[l], w3_ffn2[l], w2_ffn2[l])
        x_parts = _whole(x)

        k_p = qkv[M_PROMPT - wb:M_PROMPT, ATTN_DIM:ATTN_DIM + KV_DIM]
        v_p = qkv[M_PROMPT - wb:M_PROMPT, ATTN_DIM + KV_DIM:]
        kp.append(k_p.reshape(1, wb, N_KV_HEADS, HEAD_DIM))
        vp.append(v_p.reshape(1, wb, N_KV_HEADS, HEAD_DIM))
        last_prompt = M_PROMPT // M_SAMPLE - 1
        cp.append(tails[last_prompt, TAIL_ROWS - (CONV_WIDTH - 1):].reshape(
            1, CONV_WIDTH - 1, CONV_DIM))
        ks.append(kw_s.reshape(DEC_BATCH, wb, N_KV_HEADS, HEAD_DIM))
        vs.append(vw_s.reshape(DEC_BATCH, wb, N_KV_HEADS, HEAD_DIM))
        cs.append(tails[last_prompt + 1].reshape(
            CONV_WIDTH - 1, DEC_BATCH, CONV_DIM).transpose(1, 0, 2))

    y_prompt = _final_norm(x, g_final, 0, M_PROMPT).reshape(1, SEQ, D_MODEL)
    y_sample = _to_seq_major(_final_norm(x, g_final, M_PROMPT, M_SAMPLE))
    return (y_prompt, y_sample, jnp.stack(kp), jnp.stack(vp), jnp.stack(cp),
            jnp.stack(ks), jnp.stack(vs), jnp.stack(cs))
```

```python
import functools
import math

import numpy as np
import jax
import jax.numpy as jnp
from jax import lax
from jax.experimental import pallas as pl
from jax.experimental.pallas import tpu as pltpu

F32 = jnp.float32
BF16 = jnp.bfloat16

D_MODEL = 4096
SEQ = 8192
DEC_BATCH = 128
DEC_SEQ = 8
M_PROMPT = SEQ
M_SAMPLE = DEC_BATCH * DEC_SEQ
M_ALL = M_PROMPT + M_SAMPLE
HEAD_DIM = 128
ATTN_DIM = D_MODEL // 2
N_HEADS = ATTN_DIM // HEAD_DIM
N_KV_HEADS = 4
GQA_GROUP = N_HEADS // N_KV_HEADS
KV_DIM = N_KV_HEADS * HEAD_DIM
QKV_DIM = ATTN_DIM + 2 * KV_DIM
CONV_DIM = D_MODEL - ATTN_DIM
CONV_WIDTH = 3
WINDOW = 128
BLOCK = 128
N_BUCKETS = 32
MAX_DISTANCE = 128
D_FF = 11008
N_MOD = 9
PROJ_DIM = QKV_DIM + 3 * CONV_DIM
EPS = 1e-6
NEG = -1e30
SCALE = HEAD_DIM ** -0.5

ROW_GROUP = 128
SUBLANES = 8
NORM_ROWS = 8
VMEM_LIMIT_BYTES = 56 * 1024 * 1024


def _params(semantics):
    return pltpu.CompilerParams(dimension_semantics=semantics,
                                vmem_limit_bytes=VMEM_LIMIT_BYTES)


def _slab_index(row_block, rows_per_block):
    return (row_block < M_PROMPT // rows_per_block).astype(jnp.int32)


def _part_specs(parts, tr, cols, col_index, grid_rank):
    specs = []
    for part in parts:
        row0, rows = part[1], part[2]
        b0, nb = row0 // tr, rows // tr
        blk = lambda i, b0=b0, nb=nb: jnp.clip(i - b0, 0, nb - 1)
        if _is_seq_major(part):
            assert part[0].shape[1] == (rows // DEC_BATCH) * cols
            shape = (DEC_BATCH, (tr // DEC_BATCH) * cols)
            imap = ((lambda i, blk=blk: (0, blk(i))) if grid_rank == 1 else
                    (lambda i, j, blk=blk: (0, blk(i))))
        else:
            shape = (tr, cols)
            imap = ((lambda i, blk=blk: (blk(i), col_index())) if grid_rank == 1 else
                    (lambda i, j, blk=blk: (blk(i), col_index(j))))
        specs.append(pl.BlockSpec(shape, imap))
    return specs


def _is_seq_major(part):
    return len(part) > 3 and part[3]


def _read_part(ref, part, tr):
    v = ref[...]
    if _is_seq_major(part):
        steps = tr // DEC_BATCH
        cols = v.shape[1] // steps
        v = jnp.concatenate([v[:, t * cols:(t + 1) * cols] for t in range(steps)], axis=0)
    return v


def _pick_part(refs, parts, tr, row_block):
    v = _read_part(refs[0], parts[0], tr)
    for ref, part in zip(refs[1:], parts[1:]):
        v = jnp.where(row_block >= part[1] // tr, _read_part(ref, part, tr), v)
    return v


def _bucket_map():
    r = np.arange(BLOCK)[:, None]
    c = np.arange(2 * BLOCK)[None, :]
    n = np.maximum(BLOCK + r - c, 0)
    max_exact = N_BUCKETS // 2
    nf = np.maximum(n, 1).astype(np.float32)
    large = max_exact + (np.log(nf / max_exact) / math.log(MAX_DISTANCE / max_exact)
                         * (N_BUCKETS - max_exact)).astype(np.int32)
    large = np.minimum(large, N_BUCKETS - 1)
    return np.where(n < max_exact, n, large).astype(np.int32)


ADA_ROWS = ROW_GROUP + 16


def _silu_kernel(c_ref, o_ref):
    c = c_ref[...]
    o_ref[...] = (c / (1.0 + jnp.exp(-c))).astype(BF16)


def _silu_bf16(c):
    return pl.pallas_call(
        _silu_kernel, out_shape=jax.ShapeDtypeStruct(c.shape, BF16), name="silu")(c)


def _ada_block(a_ref, w_ref, b_ref):
    r = jnp.dot(a_ref[...], w_ref[...].astype(BF16), preferred_element_type=F32) + b_ref[...]
    prompt = jnp.broadcast_to(r[ROW_GROUP:ROW_GROUP + 1], (ROW_GROUP, r.shape[1]))
    return jnp.concatenate([r[:ROW_GROUP], prompt], axis=0)


def _ada_kernel(a_ref, w_ref, b_ref, o_ref):
    o_ref[...] = _ada_block(a_ref, w_ref, b_ref)


def _ada(a, w_ada, b_ada, n, tn=1024):
    return pl.pallas_call(
        _ada_kernel,
        out_shape=jax.ShapeDtypeStruct((2 * ROW_GROUP, n), F32),
        grid=(n // tn,),
        in_specs=[pl.BlockSpec((ADA_ROWS, D_MODEL), lambda j: (0, 0)),
                  pl.BlockSpec((D_MODEL, tn), lambda j: (0, j)),
                  pl.BlockSpec((1, tn), lambda j: (0, j))],
        out_specs=pl.BlockSpec((2 * ROW_GROUP, tn), lambda j: (0, j)),
        compiler_params=_params(("parallel",)),
        name="ada",
    )(a, w_ada, b_ada)


def _norm_mod_kernel(*refs, parts, tr):
    n = len(parts)
    g_ref, sh_ref, sc_ref, o_ref = refs[n:]
    i = pl.program_id(0)
    for r0 in range(0, tr, NORM_ROWS):
        rows = slice(r0, r0 + NORM_ROWS)
        slab_rows = slice(r0 % ROW_GROUP, r0 % ROW_GROUP + NORM_ROWS)
        x = refs[0][rows, :]
        for ref, part in zip(refs[1:n], parts[1:]):
            x = jnp.where(i >= part[1] // tr, ref[rows, :], x)
        y = x * lax.rsqrt(jnp.mean(x * x, axis=-1, keepdims=True) + EPS) * g_ref[...]
        o_ref[rows, :] = (y * (1.0 + sc_ref[slab_rows, :]) + sh_ref[slab_rows, :]).astype(BF16)


def _norm_mod(parts, g, shift, scale):
    tr = 512 // len(parts)
    slab = lambda v: pl.BlockSpec((ROW_GROUP, D_MODEL), lambda i: (_slab_index(i, tr), v[1]))
    return pl.pallas_call(
        functools.partial(_norm_mod_kernel, parts=parts, tr=tr),
        out_shape=jax.ShapeDtypeStruct((M_ALL, D_MODEL), BF16),
        grid=(M_ALL // tr,),
        in_specs=_part_specs(parts, tr, D_MODEL, lambda: 0, 1) + [
            pl.BlockSpec((1, D_MODEL), lambda i: (0, 0)), slab(shift), slab(scale)],
        out_specs=pl.BlockSpec((tr, D_MODEL), lambda i: (i, 0)),
        compiler_params=_params(("parallel",)),
        name="norm_mod",
    )(*[p[0] for p in parts], g.reshape(1, D_MODEL), shift[0], scale[0])


def _final_norm_kernel(x_ref, g_ref, o_ref):
    x = x_ref[...]
    o_ref[...] = x * lax.rsqrt(jnp.mean(x * x, axis=-1, keepdims=True) + EPS) * g_ref[...]


def _final_norm(x, g, row0, rows, tr=512, seq_major_out=False):
    if seq_major_out:
        tr = DEC_BATCH
        out_shape = jax.ShapeDtypeStruct((DEC_BATCH, (rows // DEC_BATCH) * D_MODEL), F32)
        out_spec = pl.BlockSpec((tr, D_MODEL), lambda i: (0, i))
    else:
        out_shape = jax.ShapeDtypeStruct((rows, D_MODEL), F32)
        out_spec = pl.BlockSpec((tr, D_MODEL), lambda i: (i, 0))
    return pl.pallas_call(
        _final_norm_kernel,
        out_shape=out_shape,
        grid=(rows // tr,),
        in_specs=[pl.BlockSpec((tr, D_MODEL), lambda i: (i + row0 // tr, 0)),
                  pl.BlockSpec((1, D_MODEL), lambda i: (0, 0))],
        out_specs=out_spec,
        compiler_params=_params(("parallel",)),
        name="final_norm",
    )(x, g.reshape(1, D_MODEL))


def _resident_rows(tm, k, index_map):
    return pl.BlockSpec((tm, k), index_map, pipeline_mode=pl.Buffered(1))


def _mm_plain_kernel(x_ref, w_ref, o_ref):
    o_ref[...] = jnp.dot(x_ref[...], w_ref[...].astype(BF16), preferred_element_type=F32)


def _mm_plain(x, w, n, tm=1536, tn=512):
    m, k = x.shape
    return pl.pallas_call(
        _mm_plain_kernel,
        out_shape=jax.ShapeDtypeStruct((m, n), F32),
        grid=(m // tm, n // tn),
        in_specs=[pl.BlockSpec((tm, k), lambda i, j: (i, 0)),
                  pl.BlockSpec((k, tn), lambda i, j: (0, j))],
        out_specs=pl.BlockSpec((tm, tn), lambda i, j: (i, j)),
        compiler_params=_params(("parallel", "parallel")),
        name="mm_qkv",
    )(x, w)


def _mm_swiglu_kernel(*refs, w2_steps, ada_steps):
    x_ref, w1_ref, w3_ref, w2_ref = refs[:4]
    o_ref, w2_bf16_ref = refs[-3:-1] if ada_steps else refs[-2:]
    x = x_ref[...]
    a = jnp.dot(x, w1_ref[...].astype(BF16), preferred_element_type=F32)
    b = jnp.dot(x, w3_ref[...].astype(BF16), preferred_element_type=F32)
    o_ref[...] = (a / (1.0 + jnp.exp(-a)) * b).astype(BF16)
    step = pl.program_id(0) * pl.num_programs(1) + pl.program_id(1)

    @pl.when(step < w2_steps)
    def _():
        cols = w2_bf16_ref.shape[2]
        for nb in range(w2_bf16_ref.shape[0]):
            w2_bf16_ref[nb] = w2_ref[:, nb * cols:(nb + 1) * cols].astype(BF16)

    if ada_steps:
        @pl.when(step < ada_steps)
        def _():
            refs[-1][...] = _ada_block(*refs[4:7])


W2_COLS = 512
ADA_COLS = 256


def _mm_swiglu(x, w1, w3, w2, ada=None, tm=3072, tn=256, w2_sweeps=2):
    m, k = x.shape
    n = w1.shape[1]
    nj = n // tn
    steps = (m // tm) * nj
    step = lambda i, j: i * nj + j
    w2_steps = w2_sweeps * nj
    assert steps >= w2_steps and w2.shape[0] % w2_steps == 0
    w2_rows = w2.shape[0] // w2_steps
    w2_blk = lambda i, j: jnp.minimum(step(i, j), w2_steps - 1)
    w2_nb = w2.shape[1] // W2_COLS
    in_specs = [_resident_rows(tm, k, lambda i, j: (i, 0)),
                pl.BlockSpec((k, tn), lambda i, j: (0, j)),
                pl.BlockSpec((k, tn), lambda i, j: (0, j)),
                pl.BlockSpec((w2_rows, w2.shape[1]), lambda i, j: (w2_blk(i, j), 0))]
    out_shape = [jax.ShapeDtypeStruct((m, n), BF16),
                 jax.ShapeDtypeStruct((w2_nb, w2.shape[0], W2_COLS), BF16)]
    out_specs = [pl.BlockSpec((tm, tn), lambda i, j: (i, j)),
                 pl.BlockSpec((w2_nb, w2_rows, W2_COLS), lambda i, j: (0, w2_blk(i, j), 0))]
    args = [x, w1, w3, w2]
    ada_steps = 0
    if ada is not None:
        a, w_ada, b_ada, col0, n_ada = ada
        ada_steps = n_ada // ADA_COLS
        assert steps >= ada_steps
        ada_blk = lambda i, j: jnp.minimum(step(i, j), ada_steps - 1)
        in_specs += [
            pl.BlockSpec(a.shape, lambda i, j: (0, 0), pipeline_mode=pl.Buffered(1)),
            pl.BlockSpec((k, ADA_COLS), lambda i, j: (0, col0 // ADA_COLS + ada_blk(i, j))),
            pl.BlockSpec((1, ADA_COLS), lambda i, j: (0, col0 // ADA_COLS + ada_blk(i, j)))]
        out_shape.append(jax.ShapeDtypeStruct((2 * ROW_GROUP, n_ada), F32))
        out_specs.append(pl.BlockSpec((2 * ROW_GROUP, ADA_COLS), lambda i, j: (0, ada_blk(i, j))))
        args += [a, w_ada, b_ada]
    return pl.pallas_call(
        functools.partial(_mm_swiglu_kernel, w2_steps=w2_steps, ada_steps=ada_steps),
        out_shape=tuple(out_shape),
        grid=(m // tm, nj),
        in_specs=in_specs,
        out_specs=tuple(out_specs),
        compiler_params=_params(("arbitrary", "arbitrary")),
        name="mm_swiglu",
    )(*args)


def _mm_res_kernel(*refs, x_parts, res_parts, tm, coef):
    i = pl.program_id(0)
    pos = 0
    xs = []
    for parts in x_parts:
        xs.append(_pick_part(refs[pos:pos + len(parts)], parts, tm, i))
        pos += len(parts)
    w_ref = refs[pos]
    res = _pick_part(refs[pos + 1:pos + 1 + len(res_parts)], res_parts, tm, i)
    gate_ref, o_ref = refs[-2:]
    x = xs[0] if len(xs) == 1 else jnp.concatenate(xs, axis=1)
    acc = jnp.dot(x, w_ref[...].astype(BF16), preferred_element_type=F32)
    tn = acc.shape[1]
    gate = coef * gate_ref[...]
    y = acc.reshape(tm // ROW_GROUP, ROW_GROUP, tn) * gate[None]
    o_ref[...] = res + y.reshape(tm, tn)


def _mm_res(x_parts, w, res_parts, gate, coef, tm, tn, name):
    mod, k_gate = gate
    if w.ndim == 3:
        assert w.shape[2] == tn
        k, n = w.shape[1], w.shape[0] * tn
        w_spec = pl.BlockSpec((None, k, tn), lambda i, j: (j, 0, 0))
    else:
        k, n = w.shape
        w_spec = pl.BlockSpec((k, tn), lambda i, j: (0, j))
    gate_blocks = D_MODEL // tn
    in_specs, args = [], []
    for parts in x_parts:
        in_specs += _part_specs(parts, tm, parts[0][0].shape[1], lambda j: 0, 2)
        args += [p[0] for p in parts]
    in_specs.append(w_spec)
    in_specs += _part_specs(res_parts, tm, tn, lambda j: j, 2)
    in_specs.append(pl.BlockSpec(
        (ROW_GROUP, tn), lambda i, j: (_slab_index(i, tm), k_gate * gate_blocks + j)))
    args += [w] + [p[0] for p in res_parts] + [mod]
    return pl.pallas_call(
        functools.partial(_mm_res_kernel, x_parts=x_parts, res_parts=res_parts, tm=tm,
                          coef=coef),
        out_shape=jax.ShapeDtypeStruct((M_ALL, n), F32),
        grid=(M_ALL // tm, n // tn),
        in_specs=in_specs,
        out_specs=pl.BlockSpec((tm, tn), lambda i, j: (i, j)),
        compiler_params=_params(("parallel", "parallel")),
        name=name,
    )(*args)


CONV_COLS = 256
GATE_B_BLK = QKV_DIM // CONV_COLS
GATE_C_BLK = GATE_B_BLK + CONV_DIM // CONV_COLS
CONV_H_BLK = GATE_C_BLK + CONV_DIM // CONV_COLS
TAIL_ROWS = (CONV_WIDTH - 1) * ROW_GROUP


def _mm_conv_kernel(x_ref, wb_ref, wc_ref, wh_ref, cw_ref, st_ref, yc_ref, tail_ref,
                    shift_scr, carry_scr):
    i = pl.program_id(0)
    c = pl.program_id(1)
    tm = x_ref.shape[0]
    x = x_ref[...]
    dot = lambda w_ref: jnp.dot(x, w_ref[...].astype(BF16), preferred_element_type=F32)
    gate_b = dot(wb_ref)
    u = dot(wc_ref) * dot(wh_ref)
    w0, w1, w2 = cw_ref[0:1, :], cw_ref[1:2, :], cw_ref[2:3, :]
    tail_ref[0] = u[tm - TAIL_ROWS:, :]
    n_prompt_blocks = M_PROMPT // tm

    @pl.when(i < n_prompt_blocks)
    def _():
        @pl.when(i == 0)
        def _():
            shift_scr[0:SUBLANES, :] = jnp.zeros((SUBLANES, CONV_COLS), F32)

        @pl.when(i > 0)
        def _():
            shift_scr[0:SUBLANES, :] = carry_scr[c]

        shift_scr[SUBLANES:SUBLANES + tm, :] = u
        carry_scr[c] = u[tm - SUBLANES:, :]
        u1 = shift_scr[SUBLANES - 1:SUBLANES - 1 + tm, :]
        u2 = shift_scr[SUBLANES - 2:SUBLANES - 2 + tm, :]
        yc_ref[...] = (gate_b * (u2 * w0 + u1 * w1 + u * w2)).astype(BF16)

    @pl.when(i >= n_prompt_blocks)
    def _():
        shape = (DEC_SEQ, DEC_BATCH, CONV_COLS)
        u3 = u.reshape(shape)
        st = st_ref[...]
        u1 = jnp.concatenate([st[1:2], u3[:DEC_SEQ - 1]], axis=0)
        u2 = jnp.concatenate([st[0:2], u3[:DEC_SEQ - 2]], axis=0)
        y = u2 * w0[None] + u1 * w1[None] + u3 * w2[None]
        yc_ref[...] = (gate_b.reshape(shape) * y).reshape(tm, CONV_COLS).astype(BF16)


def _mm_conv(x, w_in, conv_w, state, tm=M_SAMPLE):
    m, k = x.shape
    nc = CONV_DIM // CONV_COLS
    wcol = lambda base: pl.BlockSpec((k, CONV_COLS), lambda i, c: (0, base + c))
    return pl.pallas_call(
        _mm_conv_kernel,
        out_shape=(jax.ShapeDtypeStruct((m, CONV_DIM), BF16),
                   jax.ShapeDtypeStruct((m // tm, TAIL_ROWS, CONV_DIM), F32)),
        grid=(m // tm, nc),
        in_specs=[pl.BlockSpec((tm, k), lambda i, c: (i, 0)),
                  wcol(GATE_B_BLK), wcol(GATE_C_BLK), wcol(CONV_H_BLK),
                  pl.BlockSpec((CONV_WIDTH, CONV_COLS), lambda i, c: (0, c)),
                  pl.BlockSpec((CONV_WIDTH - 1, DEC_BATCH, CONV_COLS), lambda i, c: (0, 0, c))],
        out_specs=(pl.BlockSpec((tm, CONV_COLS), lambda i, c: (i, c)),
                   pl.BlockSpec((1, TAIL_ROWS, CONV_COLS), lambda i, c: (i, 0, c))),
        scratch_shapes=[pltpu.VMEM((tm + SUBLANES, CONV_COLS), F32),
                        pltpu.VMEM((nc, SUBLANES, CONV_COLS), F32)],
        compiler_params=_params(("arbitrary", "arbitrary")),
        name="mm_conv",
    )(x, w_in, w_in, w_in, conv_w, state)


def _build_bias(tab_ref, bucket, bias_scr):
    for h in range(N_HEADS):
        acc = jnp.zeros(bucket.shape, F32)
        for b in range(N_BUCKETS):
            acc = jnp.where(bucket == b, tab_ref[b * N_HEADS + h], acc)
        bias_scr[h] = acc


def _valid_mask(rows, has_prev):
    r = lax.broadcasted_iota(jnp.int32, (rows, 2 * BLOCK), 0)
    c = lax.broadcasted_iota(jnp.int32, (rows, 2 * BLOCK), 1)
    dist = BLOCK + r - c
    return (dist >= 0) & (dist <= WINDOW) & ((c >= BLOCK) | has_prev)


def _softmax_with_sink(s, sink):
    m = jnp.maximum(jnp.max(s, axis=-1, keepdims=True), sink)
    p = jnp.exp(s - m)
    denom = jnp.sum(p, axis=-1, keepdims=True) + jnp.exp(sink - m)
    return p / denom


def _attn_prompt_kernel(tab_ref, sink_ref, bucket_ref, q_ref, kc_ref, kp_ref, vc_ref, vp_ref,
                        o_ref, bias_scr):
    i = pl.program_id(0)

    @pl.when(i == 0)
    def _():
        _build_bias(tab_ref, bucket_ref[...], bias_scr)

    valid = _valid_mask(BLOCK, i > 0)
    for g in range(N_KV_HEADS):
        kv = slice(g * HEAD_DIM, (g + 1) * HEAD_DIM)
        heads = [g * GQA_GROUP + j for j in range(GQA_GROUP)]
        qs = jnp.concatenate(
            [q_ref[:, h * HEAD_DIM:(h + 1) * HEAD_DIM] for h in heads], axis=0).astype(BF16)
        kk = jnp.concatenate([kp_ref[:, kv], kc_ref[:, kv]], axis=0).astype(BF16)
        vv = jnp.concatenate([vp_ref[:, kv], vc_ref[:, kv]], axis=0).astype(BF16)
        s = lax.dot_general(qs, kk, (((1,), (1,)), ((), ())), preferred_element_type=F32)
        ps = []
        for j, h in enumerate(heads):
            sj = s[j * BLOCK:(j + 1) * BLOCK] * SCALE + bias_scr[h]
            sj = jnp.where(valid, sj, NEG)
            ps.append(_softmax_with_sink(sj, sink_ref[h]))
        p = jnp.concatenate(ps, axis=0).astype(BF16)
        o = jnp.dot(p, vv, preferred_element_type=F32)
        for j, h in enumerate(heads):
            o_ref[:, h * HEAD_DIM:(h + 1) * HEAD_DIM] = o[j * BLOCK:(j + 1) * BLOCK].astype(BF16)


def _attn_prompt(qkv, table, sinks, bucket):
    nb = M_PROMPT // BLOCK
    q_blk = 0
    k_blk = ATTN_DIM // KV_DIM
    v_blk = k_blk + 1
    prev = lambda i: jnp.maximum(i - 1, 0)
    smem = pl.BlockSpec(memory_space=pltpu.SMEM)
    return pl.pallas_call(
        _attn_prompt_kernel,
        out_shape=jax.ShapeDtypeStruct((M_PROMPT, ATTN_DIM), BF16),
        grid=(nb,),
        in_specs=[smem, smem,
                  pl.BlockSpec((BLOCK, 2 * BLOCK), lambda i: (0, 0)),
                  pl.BlockSpec((BLOCK, ATTN_DIM), lambda i: (i, q_blk)),
                  pl.BlockSpec((BLOCK, KV_DIM), lambda i: (i, k_blk)),
                  pl.BlockSpec((BLOCK, KV_DIM), lambda i: (prev(i), k_blk)),
                  pl.BlockSpec((BLOCK, KV_DIM), lambda i: (i, v_blk)),
                  pl.BlockSpec((BLOCK, KV_DIM), lambda i: (prev(i), v_blk))],
        out_specs=pl.BlockSpec((BLOCK, ATTN_DIM), lambda i: (i, 0)),
        scratch_shapes=[pltpu.VMEM((N_HEADS, BLOCK, 2 * BLOCK), F32)],
        compiler_params=_params(("arbitrary",)),
        name="attn_prompt",
    )(table, sinks, bucket, qkv, qkv, qkv, qkv, qkv)


def _attn_sample_kernel(tab_ref, sink_ref, bucket_ref, q_ref, kn_ref, vn_ref, ck_ref, cv_ref,
                        o_ref, kw_ref, vw_ref, bias_scr):
    i = pl.program_id(0)
    g_seq = q_ref.shape[0]
    wb = ck_ref.shape[1] // N_KV_HEADS

    @pl.when(i == 0)
    def _():
        _build_bias(tab_ref, bucket_ref[0:DEC_SEQ, :], bias_scr)

    valid = _valid_mask(DEC_SEQ, True)[None]
    pad = jnp.zeros((g_seq, 2 * BLOCK - wb - DEC_SEQ, HEAD_DIM), F32)
    for g in range(N_KV_HEADS):
        kv = slice(g * HEAD_DIM, (g + 1) * HEAD_DIM)
        head_rows = pl.ds(g, wb, stride=N_KV_HEADS)
        heads = [g * GQA_GROUP + j for j in range(GQA_GROUP)]
        qs = jnp.concatenate(
            [q_ref[:, :, h * HEAD_DIM:(h + 1) * HEAD_DIM] for h in heads], axis=1).astype(BF16)
        kk = jnp.concatenate([ck_ref[:, head_rows, :], kn_ref[:, :, kv], pad], axis=1).astype(BF16)
        vv = jnp.concatenate([cv_ref[:, head_rows, :], vn_ref[:, :, kv], pad], axis=1).astype(BF16)
        s = jnp.einsum('gqd,gkd->gqk', qs, kk, preferred_element_type=F32)
        ps = []
        for j, h in enumerate(heads):
            sj = s[:, j * DEC_SEQ:(j + 1) * DEC_SEQ] * SCALE + bias_scr[h][None]
            sj = jnp.where(valid, sj, NEG)
            ps.append(_softmax_with_sink(sj, sink_ref[h]))
        p = jnp.concatenate(ps, axis=1).astype(BF16)
        o = jnp.einsum('gqk,gkd->gqd', p, vv, preferred_element_type=F32)
        for j, h in enumerate(heads):
            o_ref[:, :, h * HEAD_DIM:(h + 1) * HEAD_DIM] = (
                o[:, j * DEC_SEQ:(j + 1) * DEC_SEQ].astype(BF16))
        new_rows = pl.ds((wb - DEC_SEQ) * N_KV_HEADS + g, DEC_SEQ, stride=N_KV_HEADS)
        kw_ref[:, new_rows, :] = kn_ref[:, :, kv]
        vw_ref[:, new_rows, :] = vn_ref[:, :, kv]
    kept = (wb - DEC_SEQ) * N_KV_HEADS
    kw_ref[:, 0:kept, :] = ck_ref[:, DEC_SEQ * N_KV_HEADS:, :]
    vw_ref[:, 0:kept, :] = cv_ref[:, DEC_SEQ * N_KV_HEADS:, :]


def _attn_sample(q, kn, vn, ck, cv, table, sinks, bucket, g_seq=8):
    rows = ck.shape[1]
    smem = pl.BlockSpec(memory_space=pltpu.SMEM)
    blk = lambda r, c: pl.BlockSpec((g_seq, r, c), lambda i: (i, 0, 0))
    return pl.pallas_call(
        _attn_sample_kernel,
        out_shape=(jax.ShapeDtypeStruct((DEC_BATCH, DEC_SEQ, ATTN_DIM), BF16),
                   jax.ShapeDtypeStruct((DEC_BATCH, rows, HEAD_DIM), F32),
                   jax.ShapeDtypeStruct((DEC_BATCH, rows, HEAD_DIM), F32)),
        grid=(DEC_BATCH // g_seq,),
        in_specs=[smem, smem,
                  pl.BlockSpec((BLOCK, 2 * BLOCK), lambda i: (0, 0)),
                  blk(DEC_SEQ, ATTN_DIM), blk(DEC_SEQ, KV_DIM), blk(DEC_SEQ, KV_DIM),
                  blk(rows, HEAD_DIM), blk(rows, HEAD_DIM)],
        out_specs=(blk(DEC_SEQ, ATTN_DIM), blk(rows, HEAD_DIM), blk(rows, HEAD_DIM)),
        scratch_shapes=[pltpu.VMEM((N_HEADS, DEC_SEQ, 2 * BLOCK), F32)],
        compiler_params=_params(("arbitrary",)),
        name="attn_sample",
    )(table, sinks, bucket, q, kn, vn, ck, cv)


def _to_step_major(a):
    return a.transpose(1, 0, 2).reshape(M_SAMPLE, a.shape[-1])


def _to_seq_major(a):
    return a.reshape(DEC_SEQ, DEC_BATCH, a.shape[-1]).transpose(1, 0, 2)


def _whole(a):
    return [(a, 0, M_ALL)]


N_MOD_EARLY = 2


def _ffn(x_parts, mods, k0, g, w1, w3, w2, ada_rest=None):
    h = _norm_mod(x_parts, g, mods[k0], mods[k0 + 1])
    if ada_rest is None:
        u, w2_bf16 = _mm_swiglu(h, w1, w3, w2)
    else:
        n_rest = (N_MOD - N_MOD_EARLY) * D_MODEL
        u, w2_bf16, mod_rest = _mm_swiglu(
            h, w1, w3, w2, ada=ada_rest + (N_MOD_EARLY * D_MODEL, n_rest), tm=2304,
            w2_sweeps=4)
        mods += [(mod_rest, k) for k in range(N_MOD - N_MOD_EARLY)]
    return _mm_res([_whole(u)], w2_bf16, x_parts, mods[k0 + 2], 0.5, tm=512, tn=512,
                   name="mm_w2")


def kernel(x_prompt, x_sample, c_prompt, c_sample, cache_k, cache_v, state_conv, rel_bias,
           g_ffn1, w1_ffn1, w3_ffn1, w2_ffn1, g_mix, w_in, sinks, conv_w, w_out,
           g_ffn2, w1_ffn2, w3_ffn2, w2_ffn2, w_ada, b_ada, g_final):
    depth = w_in.shape[0]
    wb = cache_k.shape[2]
    bucket = jnp.asarray(_bucket_map())
    table = rel_bias.reshape(N_BUCKETS * N_HEADS)
    c_rows = jnp.concatenate(
        [c_sample, c_prompt, jnp.zeros((ADA_ROWS - ROW_GROUP - 1, D_MODEL), F32)], axis=0)
    silu_c = _silu_bf16(c_rows)

    x_parts = [(x_prompt.reshape(M_PROMPT, D_MODEL), 0, M_PROMPT),
               (_to_step_major(x_sample), M_PROMPT, M_SAMPLE)]

    kp, vp, cp, ks, vs, cs = [], [], [], [], [], []
    for l in range(depth):
        b_ada_l = b_ada[l].reshape(1, N_MOD * D_MODEL)
        mod_early = _ada(silu_c, w_ada[l], b_ada_l, N_MOD_EARLY * D_MODEL)
        mods = [(mod_early, k) for k in range(N_MOD_EARLY)]
        x = _ffn(x_parts, mods, 0, g_ffn1[l], w1_ffn1[l], w3_ffn1[l], w2_ffn1[l],
                 ada_rest=(silu_c, w_ada[l], b_ada_l))

        h = _norm_mod(_whole(x), g_mix[l], mods[3], mods[4])
        qkv = _mm_plain(h, w_in[l], QKV_DIM)
        yc, tails = _mm_conv(h, w_in[l], conv_w[l], state_conv[l].transpose(1, 0, 2))

        o_p = _attn_prompt(qkv, table, sinks[l], bucket)
        qkv_s = qkv[M_PROMPT:]
        q_s = _to_seq_major(qkv_s[:, :ATTN_DIM])
        kn_s = _to_seq_major(qkv_s[:, ATTN_DIM:ATTN_DIM + KV_DIM])
        vn_s = _to_seq_major(qkv_s[:, ATTN_DIM + KV_DIM:])
        o_s, kw_s, vw_s = _attn_sample(
            q_s, kn_s, vn_s,
            cache_k[l].reshape(DEC_BATCH, wb * N_KV_HEADS, HEAD_DIM),
            cache_v[l].reshape(DEC_BATCH, wb * N_KV_HEADS, HEAD_DIM), table, sinks[l], bucket)
        o_parts = [(o_p, 0, M_PROMPT),
                   (o_s.reshape(DEC_BATCH, DEC_SEQ * ATTN_DIM), M_PROMPT, M_SAMPLE, True)]
        x = _mm_res([o_parts, _whole(yc)], w_out[l], _whole(x), mods[5], 1.0, tm=1024, tn=512,
                    name="mm_out")

        x = _ffn(_whole(x), mods, 6, g_ffn2[l], w1_ffn2[l], w3_ffn2[l], w2_ffn2[l])
        x_parts = _whole(x)

        k_p = qkv[M_PROMPT - wb:M_PROMPT, ATTN_DIM:ATTN_DIM + KV_DIM]
        v_p = qkv[M_PROMPT - wb:M_PROMPT, ATTN_DIM + KV_DIM:]
        kp.append(k_p.reshape(1, wb, N_KV_HEADS, HEAD_DIM))
        vp.append(v_p.reshape(1, wb, N_KV_HEADS, HEAD_DIM))
        last_prompt = M_PROMPT // M_SAMPLE - 1
        cp.append(tails[last_prompt, TAIL_ROWS - (CONV_WIDTH - 1):].reshape(
            1, CONV_WIDTH - 1, CONV_DIM))
        ks.append(kw_s.reshape(DEC_BATCH, wb, N_KV_HEADS, HEAD_DIM))
        vs.append(vw_s.reshape(DEC_BATCH, wb, N_KV_HEADS, HEAD_DIM))
        cs.append(tails[last_prompt + 1].reshape(
            CONV_WIDTH - 1, DEC_BATCH, CONV_DIM).transpose(1, 0, 2))

    y_prompt = _final_norm(x, g_final, 0, M_PROMPT).reshape(1, SEQ, D_MODEL)
    y_sample = _final_norm(x, g_final, M_PROMPT, M_SAMPLE, seq_major_out=True).reshape(
        DEC_BATCH, DEC_SEQ, D_MODEL)
    return (y_prompt, y_sample, jnp.stack(kp), jnp.stack(vp), jnp.stack(cp),
            jnp.stack(ks), jnp.stack(vs), jnp.stack(cs))
```

```python
import functools
import math

import numpy as np
import jax
import jax.numpy as jnp
from jax import lax
from jax.experimental import pallas as pl
from jax.experimental.pallas import tpu as pltpu

F32 = jnp.float32
BF16 = jnp.bfloat16

D_MODEL = 4096
SEQ = 8192
DEC_BATCH = 128
DEC_SEQ = 8
M_PROMPT = SEQ
M_SAMPLE = DEC_BATCH * DEC_SEQ
M_ALL = M_PROMPT + M_SAMPLE
HEAD_DIM = 128
ATTN_DIM = D_MODEL // 2
N_HEADS = ATTN_DIM // HEAD_DIM
N_KV_HEADS = 4
GQA_GROUP = N_HEADS // N_KV_HEADS
KV_DIM = N_KV_HEADS * HEAD_DIM
QKV_DIM = ATTN_DIM + 2 * KV_DIM
CONV_DIM = D_MODEL - ATTN_DIM
CONV_WIDTH = 3
WINDOW = 128
BLOCK = 128
N_BUCKETS = 32
MAX_DISTANCE = 128
D_FF = 11008
N_MOD = 9
PROJ_DIM = QKV_DIM + 3 * CONV_DIM
EPS = 1e-6
NEG = -1e30
SCALE = HEAD_DIM ** -0.5

ROW_GROUP = 128
SUBLANES = 8
NORM_ROWS = 8
VMEM_LIMIT_BYTES = 56 * 1024 * 1024


def _params(semantics):
    return pltpu.CompilerParams(dimension_semantics=semantics,
                                vmem_limit_bytes=VMEM_LIMIT_BYTES)


def _slab_index(row_block, rows_per_block):
    return (row_block < M_PROMPT // rows_per_block).astype(jnp.int32)


def _part_specs(parts, tr, cols, col_index, grid_rank):
    specs = []
    for part in parts:
        row0, rows = part[1], part[2]
        b0, nb = row0 // tr, rows // tr
        blk = lambda i, b0=b0, nb=nb: jnp.clip(i - b0, 0, nb - 1)
        if _is_seq_major(part):
            assert part[0].shape[1] == (rows // DEC_BATCH) * cols
            shape = (DEC_BATCH, (tr // DEC_BATCH) * cols)
            imap = ((lambda i, blk=blk: (0, blk(i))) if grid_rank == 1 else
                    (lambda i, j, blk=blk: (0, blk(i))))
        else:
            shape = (tr, cols)
            imap = ((lambda i, blk=blk: (blk(i), col_index())) if grid_rank == 1 else
                    (lambda i, j, blk=blk: (blk(i), col_index(j))))
        specs.append(pl.BlockSpec(shape, imap))
    return specs


def _is_seq_major(part):
    return len(part) > 3 and part[3]


def _read_part(ref, part, tr):
    v = ref[...]
    if _is_seq_major(part):
        steps = tr // DEC_BATCH
        cols = v.shape[1] // steps
        v = jnp.concatenate([v[:, t * cols:(t + 1) * cols] for t in range(steps)], axis=0)
    return v


def _pick_part(refs, parts, tr, row_block):
    v = _read_part(refs[0], parts[0], tr)
    for ref, part in zip(refs[1:], parts[1:]):
        v = jnp.where(row_block >= part[1] // tr, _read_part(ref, part, tr), v)
    return v


def _bucket_map():
    r = np.arange(BLOCK)[:, None]
    c = np.arange(2 * BLOCK)[None, :]
    n = np.maximum(BLOCK + r - c, 0)
    max_exact = N_BUCKETS // 2
    nf = np.maximum(n, 1).astype(np.float32)
    large = max_exact + (np.log(nf / max_exact) / math.log(MAX_DISTANCE / max_exact)
                         * (N_BUCKETS - max_exact)).astype(np.int32)
    large = np.minimum(large, N_BUCKETS - 1)
    return np.where(n < max_exact, n, large).astype(np.int32)


ADA_ROWS = ROW_GROUP + 16


def _silu_kernel(c_ref, o_ref):
    c = c_ref[...]
    o_ref[...] = (c / (1.0 + jnp.exp(-c))).astype(BF16)


def _silu_bf16(c):
    return pl.pallas_call(
        _silu_kernel, out_shape=jax.ShapeDtypeStruct(c.shape, BF16), name="silu")(c)


def _ada_block(a_ref, w_ref, b_ref):
    r = jnp.dot(a_ref[...], w_ref[...].astype(BF16), preferred_element_type=F32) + b_ref[...]
    prompt = jnp.broadcast_to(r[ROW_GROUP:ROW_GROUP + 1], (ROW_GROUP, r.shape[1]))
    return jnp.concatenate([r[:ROW_GROUP], prompt], axis=0)


def _ada_kernel(a_ref, w_ref, b_ref, o_ref):
    o_ref[...] = _ada_block(a_ref, w_ref, b_ref)


def _ada(a, w_ada, b_ada, n, tn=1024):
    return pl.pallas_call(
        _ada_kernel,
        out_shape=jax.ShapeDtypeStruct((2 * ROW_GROUP, n), F32),
        grid=(n // tn,),
        in_specs=[pl.BlockSpec((ADA_ROWS, D_MODEL), lambda j: (0, 0)),
                  pl.BlockSpec((D_MODEL, tn), lambda j: (0, j)),
                  pl.BlockSpec((1, tn), lambda j: (0, j))],
        out_specs=pl.BlockSpec((2 * ROW_GROUP, tn), lambda j: (0, j)),
        compiler_params=_params(("parallel",)),
        name="ada",
    )(a, w_ada, b_ada)


def _norm_mod_kernel(*refs, parts, tr):
    n = len(parts)
    g_ref, sh_ref, sc_ref, o_ref = refs[n:]
    i = pl.program_id(0)
    for r0 in range(0, tr, NORM_ROWS):
        rows = slice(r0, r0 + NORM_ROWS)
        slab_rows = slice(r0 % ROW_GROUP, r0 % ROW_GROUP + NORM_ROWS)
        x = refs[0][rows, :]
        for ref, part in zip(refs[1:n], parts[1:]):
            x = jnp.where(i >= part[1] // tr, ref[rows, :], x)
        y = x * lax.rsqrt(jnp.mean(x * x, axis=-1, keepdims=True) + EPS) * g_ref[...]
        o_ref[rows, :] = (y * (1.0 + sc_ref[slab_rows, :]) + sh_ref[slab_rows, :]).astype(BF16)


def _norm_mod(parts, g, shift, scale):
    tr = 512 // len(parts)
    slab = lambda v: pl.BlockSpec((ROW_GROUP, D_MODEL), lambda i: (_slab_index(i, tr), v[1]))
    return pl.pallas_call(
        functools.partial(_norm_mod_kernel, parts=parts, tr=tr),
        out_shape=jax.ShapeDtypeStruct((M_ALL, D_MODEL), BF16),
        grid=(M_ALL // tr,),
        in_specs=_part_specs(parts, tr, D_MODEL, lambda: 0, 1) + [
            pl.BlockSpec((1, D_MODEL), lambda i: (0, 0)), slab(shift), slab(scale)],
        out_specs=pl.BlockSpec((tr, D_MODEL), lambda i: (i, 0)),
        compiler_params=_params(("parallel",)),
        name="norm_mod",
    )(*[p[0] for p in parts], g.reshape(1, D_MODEL), shift[0], scale[0])


def _final_norm_kernel(x_ref, g_ref, o_ref):
    x = x_ref[...]
    o_ref[...] = x * lax.rsqrt(jnp.mean(x * x, axis=-1, keepdims=True) + EPS) * g_ref[...]


def _final_norm(x, g, row0, rows, tr=512, seq_major_out=False):
    if seq_major_out:
        tr = DEC_BATCH
        out_shape = jax.ShapeDtypeStruct((DEC_BATCH, (rows // DEC_BATCH) * D_MODEL), F32)
        out_spec = pl.BlockSpec((tr, D_MODEL), lambda i: (0, i))
    else:
        out_shape = jax.ShapeDtypeStruct((rows, D_MODEL), F32)
        out_spec = pl.BlockSpec((tr, D_MODEL), lambda i: (i, 0))
    return pl.pallas_call(
        _final_norm_kernel,
        out_shape=out_shape,
        grid=(rows // tr,),
        in_specs=[pl.BlockSpec((tr, D_MODEL), lambda i: (i + row0 // tr, 0)),
                  pl.BlockSpec((1, D_MODEL), lambda i: (0, 0))],
        out_specs=out_spec,
        compiler_params=_params(("parallel",)),
        name="final_norm",
    )(x, g.reshape(1, D_MODEL))


def _resident_rows(tm, k, index_map):
    return pl.BlockSpec((tm, k), index_map, pipeline_mode=pl.Buffered(1))


def _grid_step():
    return pl.program_id(0) * pl.num_programs(1) + pl.program_id(1)


def _rounder_specs(rounders, n_inner, total_steps):
    in_specs, out_shapes, out_specs, static = [], [], [], []
    for src, col0, n_cols, cols, steps in rounders:
        assert total_steps >= steps and src.shape[0] % steps == 0 and n_cols % cols == 0
        rows = src.shape[0] // steps
        blk = lambda i, j, steps=steps: jnp.minimum(i * n_inner + j, steps - 1)
        in_specs.append(pl.BlockSpec((rows, src.shape[1]), lambda i, j, blk=blk: (blk(i, j), 0)))
        out_shapes.append(jax.ShapeDtypeStruct((n_cols // cols, src.shape[0], cols), BF16))
        out_specs.append(pl.BlockSpec((n_cols // cols, rows, cols),
                                      lambda i, j, blk=blk: (0, blk(i, j), 0)))
        static.append((col0, steps))
    return in_specs, out_shapes, out_specs, tuple(static)


def _run_rounders(static, src_refs, dst_refs):
    step = _grid_step()
    for (col0, steps), src_ref, dst_ref in zip(static, src_refs, dst_refs):
        @pl.when(step < steps)
        def _(col0=col0, src_ref=src_ref, dst_ref=dst_ref):
            cols = dst_ref.shape[2]
            for nb in range(dst_ref.shape[0]):
                lo = col0 + nb * cols
                dst_ref[nb] = src_ref[:, lo:lo + cols].astype(BF16)


def _mm_plain_kernel(*refs, rounders):
    n_r = len(rounders)
    x_ref, w_ref = refs[:2]
    o_ref = refs[2 + n_r]
    o_ref[...] = jnp.dot(x_ref[...], w_ref[...].astype(BF16), preferred_element_type=F32)
    _run_rounders(rounders, refs[2:2 + n_r], refs[3 + n_r:])


def _mm_plain(x, w, n, rounders=(), tm=1536, tn=512):
    m, k = x.shape
    nj = n // tn
    r_in, r_shapes, r_out, r_static = _rounder_specs(rounders, nj, (m // tm) * nj)
    return pl.pallas_call(
        functools.partial(_mm_plain_kernel, rounders=r_static),
        out_shape=(jax.ShapeDtypeStruct((m, n), F32), *r_shapes),
        grid=(m // tm, nj),
        in_specs=[pl.BlockSpec((tm, k), lambda i, j: (i, 0)),
                  pl.BlockSpec((k, tn), lambda i, j: (0, j))] + r_in,
        out_specs=(pl.BlockSpec((tm, tn), lambda i, j: (i, j)), *r_out),
        compiler_params=_params(("arbitrary", "arbitrary")),
        name="mm_qkv",
    )(x, w, *[r[0] for r in rounders])


def _mm_swiglu_kernel(*refs, rounders, ada_steps):
    n_r = len(rounders)
    n_ada = 3 if ada_steps else 0
    x_ref, w1_ref, w3_ref = refs[:3]
    n_in = 3 + n_ada + n_r
    o_ref = refs[n_in]
    x = x_ref[...]
    a = jnp.dot(x, w1_ref[...].astype(BF16), preferred_element_type=F32)
    b = jnp.dot(x, w3_ref[...].astype(BF16), preferred_element_type=F32)
    o_ref[...] = (a / (1.0 + jnp.exp(-a)) * b).astype(BF16)
    _run_rounders(rounders, refs[3 + n_ada:n_in], refs[len(refs) - n_r:])

    if ada_steps:
        @pl.when(_grid_step() < ada_steps)
        def _():
            refs[n_in + 1][...] = _ada_block(*refs[3:6])


W2_COLS = 512
ADA_COLS = 256


def _mm_swiglu(x, w1, w3, rounders=(), ada=None, tm=3072, tn=256):
    m, k = x.shape
    n = w1.shape[1]
    nj = n // tn
    steps = (m // tm) * nj
    in_specs = [_resident_rows(tm, k, lambda i, j: (i, 0)),
                pl.BlockSpec((k, tn), lambda i, j: (0, j)),
                pl.BlockSpec((k, tn), lambda i, j: (0, j))]
    out_shape = [jax.ShapeDtypeStruct((m, n), BF16)]
    out_specs = [pl.BlockSpec((tm, tn), lambda i, j: (i, j))]
    args = [x, w1, w3]
    ada_steps = 0
    if ada is not None:
        a, w_ada, b_ada, col0, n_ada = ada
        ada_steps = n_ada // ADA_COLS
        assert steps >= ada_steps
        ada_blk = lambda i, j: jnp.minimum(i * nj + j, ada_steps - 1)
        in_specs += [
            pl.BlockSpec(a.shape, lambda i, j: (0, 0), pipeline_mode=pl.Buffered(1)),
            pl.BlockSpec((k, ADA_COLS), lambda i, j: (0, col0 // ADA_COLS + ada_blk(i, j))),
            pl.BlockSpec((1, ADA_COLS), lambda i, j: (0, col0 // ADA_COLS + ada_blk(i, j)))]
        out_shape.append(jax.ShapeDtypeStruct((2 * ROW_GROUP, n_ada), F32))
        out_specs.append(pl.BlockSpec((2 * ROW_GROUP, ADA_COLS), lambda i, j: (0, ada_blk(i, j))))
        args += [a, w_ada, b_ada]
    r_in, r_shapes, r_out, r_static = _rounder_specs(rounders, nj, steps)
    return pl.pallas_call(
        functools.partial(_mm_swiglu_kernel, rounders=r_static, ada_steps=ada_steps),
        out_shape=(*out_shape, *r_shapes),
        grid=(m // tm, nj),
        in_specs=in_specs + r_in,
        out_specs=(*out_specs, *r_out),
        compiler_params=_params(("arbitrary", "arbitrary")),
        name="mm_swiglu",
    )(*args, *[r[0] for r in rounders])


def _mm_res_kernel(*refs, x_parts, res_parts, tm, coef, rounders):
    i = pl.program_id(0)
    pos = 0
    xs = []
    for parts in x_parts:
        xs.append(_pick_part(refs[pos:pos + len(parts)], parts, tm, i))
        pos += len(parts)
    w_ref = refs[pos]
    res = _pick_part(refs[pos + 1:pos + 1 + len(res_parts)], res_parts, tm, i)
    gate_ref = refs[pos + 1 + len(res_parts)]
    n_in = pos + 2 + len(res_parts) + len(rounders)
    o_ref = refs[n_in]
    x = xs[0] if len(xs) == 1 else jnp.concatenate(xs, axis=1)
    acc = jnp.dot(x, w_ref[...].astype(BF16), preferred_element_type=F32)
    tn = acc.shape[1]
    gate = coef * gate_ref[...]
    y = acc.reshape(tm // ROW_GROUP, ROW_GROUP, tn) * gate[None]
    o_ref[...] = res + y.reshape(tm, tn)
    _run_rounders(rounders, refs[n_in - len(rounders):n_in], refs[n_in + 1:])


def _mm_res(x_parts, w, res_parts, gate, coef, tm, tn, name, rounders=()):
    mod, k_gate = gate
    if w.ndim == 3:
        assert w.shape[2] == tn
        k, n = w.shape[1], w.shape[0] * tn
        w_spec = pl.BlockSpec((None, k, tn), lambda i, j: (j, 0, 0))
    else:
        k, n = w.shape
        w_spec = pl.BlockSpec((k, tn), lambda i, j: (0, j))
    gate_blocks = D_MODEL // tn
    nj = n // tn
    in_specs, args = [], []
    for parts in x_parts:
        in_specs += _part_specs(parts, tm, parts[0][0].shape[1], lambda j: 0, 2)
        args += [p[0] for p in parts]
    in_specs.append(w_spec)
    in_specs += _part_specs(res_parts, tm, tn, lambda j: j, 2)
    in_specs.append(pl.BlockSpec(
        (ROW_GROUP, tn), lambda i, j: (_slab_index(i, tm), k_gate * gate_blocks + j)))
    args += [w] + [p[0] for p in res_parts] + [mod]
    r_in, r_shapes, r_out, r_static = _rounder_specs(rounders, nj, (M_ALL // tm) * nj)
    return pl.pallas_call(
        functools.partial(_mm_res_kernel, x_parts=x_parts, res_parts=res_parts, tm=tm,
                          coef=coef, rounders=r_static),
        out_shape=(jax.ShapeDtypeStruct((M_ALL, n), F32), *r_shapes),
        grid=(M_ALL // tm, nj),
        in_specs=in_specs + r_in,
        out_specs=(pl.BlockSpec((tm, tn), lambda i, j: (i, j)), *r_out),
        compiler_params=_params(("arbitrary", "arbitrary")),
        name=name,
    )(*args, *[r[0] for r in rounders])


CONV_COLS = 256
TAIL_ROWS = (CONV_WIDTH - 1) * ROW_GROUP


def _mm_conv_kernel(x_ref, wb_ref, wc_ref, wh_ref, cw_ref, st_ref, yc_ref, tail_ref,
                    shift_scr, carry_scr):
    i = pl.program_id(0)
    c = pl.program_id(1)
    tm = x_ref.shape[0]
    x = x_ref[...]
    dot = lambda w_ref: jnp.dot(x, w_ref[...].astype(BF16), preferred_element_type=F32)
    gate_b = dot(wb_ref)
    u = dot(wc_ref) * dot(wh_ref)
    w0, w1, w2 = cw_ref[0:1, :], cw_ref[1:2, :], cw_ref[2:3, :]
    tail_ref[0] = u[tm - TAIL_ROWS:, :]
    n_prompt_blocks = M_PROMPT // tm

    @pl.when(i < n_prompt_blocks)
    def _():
        @pl.when(i == 0)
        def _():
            shift_scr[0:SUBLANES, :] = jnp.zeros((SUBLANES, CONV_COLS), F32)

        @pl.when(i > 0)
        def _():
            shift_scr[0:SUBLANES, :] = carry_scr[c]

        shift_scr[SUBLANES:SUBLANES + tm, :] = u
        carry_scr[c] = u[tm - SUBLANES:, :]
        u1 = shift_scr[SUBLANES - 1:SUBLANES - 1 + tm, :]
        u2 = shift_scr[SUBLANES - 2:SUBLANES - 2 + tm, :]
        yc_ref[...] = (gate_b * (u2 * w0 + u1 * w1 + u * w2)).astype(BF16)

    @pl.when(i >= n_prompt_blocks)
    def _():
        shape = (DEC_SEQ, DEC_BATCH, CONV_COLS)
        u3 = u.reshape(shape)
        st = st_ref[...]
        u1 = jnp.concatenate([st[1:2], u3[:DEC_SEQ - 1]], axis=0)
        u2 = jnp.concatenate([st[0:2], u3[:DEC_SEQ - 2]], axis=0)
        y = u2 * w0[None] + u1 * w1[None] + u3 * w2[None]
        yc_ref[...] = (gate_b.reshape(shape) * y).reshape(tm, CONV_COLS).astype(BF16)


def _mm_conv(x, w_conv, conv_w, state, tm=M_SAMPLE):
    m, k = x.shape
    nc = CONV_DIM // CONV_COLS
    wcol = lambda base: pl.BlockSpec((None, k, CONV_COLS), lambda i, c: (base + c, 0, 0))
    return pl.pallas_call(
        _mm_conv_kernel,
        out_shape=(jax.ShapeDtypeStruct((m, CONV_DIM), BF16),
                   jax.ShapeDtypeStruct((m // tm, TAIL_ROWS, CONV_DIM), F32)),
        grid=(m // tm, nc),
        in_specs=[pl.BlockSpec((tm, k), lambda i, c: (i, 0)),
                  wcol(0), wcol(nc), wcol(2 * nc),
                  pl.BlockSpec((CONV_WIDTH, CONV_COLS), lambda i, c: (0, c)),
                  pl.BlockSpec((CONV_WIDTH - 1, DEC_BATCH, CONV_COLS), lambda i, c: (0, 0, c))],
        out_specs=(pl.BlockSpec((tm, CONV_COLS), lambda i, c: (i, c)),
                   pl.BlockSpec((1, TAIL_ROWS, CONV_COLS), lambda i, c: (i, 0, c))),
        scratch_shapes=[pltpu.VMEM((tm + SUBLANES, CONV_COLS), F32),
                        pltpu.VMEM((nc, SUBLANES, CONV_COLS), F32)],
        compiler_params=_params(("arbitrary", "arbitrary")),
        name="mm_conv",
    )(x, w_conv, w_conv, w_conv, conv_w, state)


def _build_bias(tab_ref, bucket, bias_scr):
    for h in range(N_HEADS):
        acc = jnp.zeros(bucket.shape, F32)
        for b in range(N_BUCKETS):
            acc = jnp.where(bucket == b, tab_ref[b * N_HEADS + h], acc)
        bias_scr[h] = acc


def _valid_mask(rows, has_prev):
    r = lax.broadcasted_iota(jnp.int32, (rows, 2 * BLOCK), 0)
    c = lax.broadcasted_iota(jnp.int32, (rows, 2 * BLOCK), 1)
    dist = BLOCK + r - c
    return (dist >= 0) & (dist <= WINDOW) & ((c >= BLOCK) | has_prev)


def _softmax_with_sink(s, sink):
    m = jnp.maximum(jnp.max(s, axis=-1, keepdims=True), sink)
    p = jnp.exp(s - m)
    denom = jnp.sum(p, axis=-1, keepdims=True) + jnp.exp(sink - m)
    return p / denom


def _attn_prompt_kernel(tab_ref, sink_ref, bucket_ref, q_ref, kc_ref, kp_ref, vc_ref, vp_ref,
                        o_ref, bias_scr):
    i = pl.program_id(0)

    @pl.when(i == 0)
    def _():
        _build_bias(tab_ref, bucket_ref[...], bias_scr)

    valid = _valid_mask(BLOCK, i > 0)
    for g in range(N_KV_HEADS):
        kv = slice(g * HEAD_DIM, (g + 1) * HEAD_DIM)
        heads = [g * GQA_GROUP + j for j in range(GQA_GROUP)]
        qs = jnp.concatenate(
            [q_ref[:, h * HEAD_DIM:(h + 1) * HEAD_DIM] for h in heads], axis=0).astype(BF16)
        kk = jnp.concatenate([kp_ref[:, kv], kc_ref[:, kv]], axis=0).astype(BF16)
        vv = jnp.concatenate([vp_ref[:, kv], vc_ref[:, kv]], axis=0).astype(BF16)
        s = lax.dot_general(qs, kk, (((1,), (1,)), ((), ())), preferred_element_type=F32)
        ps = []
        for j, h in enumerate(heads):
            sj = s[j * BLOCK:(j + 1) * BLOCK] * SCALE + bias_scr[h]
            sj = jnp.where(valid, sj, NEG)
            ps.append(_softmax_with_sink(sj, sink_ref[h]))
        p = jnp.concatenate(ps, axis=0).astype(BF16)
        o = jnp.dot(p, vv, preferred_element_type=F32)
        for j, h in enumerate(heads):
            o_ref[:, h * HEAD_DIM:(h + 1) * HEAD_DIM] = o[j * BLOCK:(j + 1) * BLOCK].astype(BF16)


def _attn_prompt(qkv, table, sinks, bucket):
    nb = M_PROMPT // BLOCK
    q_blk = 0
    k_blk = ATTN_DIM // KV_DIM
    v_blk = k_blk + 1
    prev = lambda i: jnp.maximum(i - 1, 0)
    smem = pl.BlockSpec(memory_space=pltpu.SMEM)
    return pl.pallas_call(
        _attn_prompt_kernel,
        out_shape=jax.ShapeDtypeStruct((M_PROMPT, ATTN_DIM), BF16),
        grid=(nb,),
        in_specs=[smem, smem,
                  pl.BlockSpec((BLOCK, 2 * BLOCK), lambda i: (0, 0)),
                  pl.BlockSpec((BLOCK, ATTN_DIM), lambda i: (i, q_blk)),
                  pl.BlockSpec((BLOCK, KV_DIM), lambda i: (i, k_blk)),
                  pl.BlockSpec((BLOCK, KV_DIM), lambda i: (prev(i), k_blk)),
                  pl.BlockSpec((BLOCK, KV_DIM), lambda i: (i, v_blk)),
                  pl.BlockSpec((BLOCK, KV_DIM), lambda i: (prev(i), v_blk))],
        out_specs=pl.BlockSpec((BLOCK, ATTN_DIM), lambda i: (i, 0)),
        scratch_shapes=[pltpu.VMEM((N_HEADS, BLOCK, 2 * BLOCK), F32)],
        compiler_params=_params(("arbitrary",)),
        name="attn_prompt",
    )(table, sinks, bucket, qkv, qkv, qkv, qkv, qkv)


def _attn_sample_kernel(tab_ref, sink_ref, bucket_ref, q_ref, kn_ref, vn_ref, ck_ref, cv_ref,
                        o_ref, kw_ref, vw_ref, bias_scr):
    i = pl.program_id(0)
    g_seq = q_ref.shape[0]
    wb = ck_ref.shape[1] // N_KV_HEADS

    @pl.when(i == 0)
    def _():
        _build_bias(tab_ref, bucket_ref[0:DEC_SEQ, :], bias_scr)

    valid = _valid_mask(DEC_SEQ, True)[None]
    pad = jnp.zeros((g_seq, 2 * BLOCK - wb - DEC_SEQ, HEAD_DIM), F32)
    for g in range(N_KV_HEADS):
        kv = slice(g * HEAD_DIM, (g + 1) * HEAD_DIM)
        head_rows = pl.ds(g, wb, stride=N_KV_HEADS)
        heads = [g * GQA_GROUP + j for j in range(GQA_GROUP)]
        qs = jnp.concatenate(
            [q_ref[:, :, h * HEAD_DIM:(h + 1) * HEAD_DIM] for h in heads], axis=1).astype(BF16)
        kk = jnp.concatenate([ck_ref[:, head_rows, :], kn_ref[:, :, kv], pad], axis=1).astype(BF16)
        vv = jnp.concatenate([cv_ref[:, head_rows, :], vn_ref[:, :, kv], pad], axis=1).astype(BF16)
        s = jnp.einsum('gqd,gkd->gqk', qs, kk, preferred_element_type=F32)
        ps = []
        for j, h in enumerate(heads):
            sj = s[:, j * DEC_SEQ:(j + 1) * DEC_SEQ] * SCALE + bias_scr[h][None]
            sj = jnp.where(valid, sj, NEG)
            ps.append(_softmax_with_sink(sj, sink_ref[h]))
        p = jnp.concatenate(ps, axis=1).astype(BF16)
        o = jnp.einsum('gqk,gkd->gqd', p, vv, preferred_element_type=F32)
        for j, h in enumerate(heads):
            o_ref[:, :, h * HEAD_DIM:(h + 1) * HEAD_DIM] = (
                o[:, j * DEC_SEQ:(j + 1) * DEC_SEQ].astype(BF16))
        new_rows = pl.ds((wb - DEC_SEQ) * N_KV_HEADS + g, DEC_SEQ, stride=N_KV_HEADS)
        kw_ref[:, new_rows, :] = kn_ref[:, :, kv]
        vw_ref[:, new_rows, :] = vn_ref[:, :, kv]
    kept = (wb - DEC_SEQ) * N_KV_HEADS
    kw_ref[:, 0:kept, :] = ck_ref[:, DEC_SEQ * N_KV_HEADS:, :]
    vw_ref[:, 0:kept, :] = cv_ref[:, DEC_SEQ * N_KV_HEADS:, :]


def _attn_sample(q, kn, vn, ck, cv, table, sinks, bucket, g_seq=8):
    rows = ck.shape[1]
    smem = pl.BlockSpec(memory_space=pltpu.SMEM)
    blk = lambda r, c: pl.BlockSpec((g_seq, r, c), lambda i: (i, 0, 0))
    return pl.pallas_call(
        _attn_sample_kernel,
        out_shape=(jax.ShapeDtypeStruct((DEC_BATCH, DEC_SEQ, ATTN_DIM), BF16),
                   jax.ShapeDtypeStruct((DEC_BATCH, rows, HEAD_DIM), F32),
                   jax.ShapeDtypeStruct((DEC_BATCH, rows, HEAD_DIM), F32)),
        grid=(DEC_BATCH // g_seq,),
        in_specs=[smem, smem,
                  pl.BlockSpec((BLOCK, 2 * BLOCK), lambda i: (0, 0)),
                  blk(DEC_SEQ, ATTN_DIM), blk(DEC_SEQ, KV_DIM), blk(DEC_SEQ, KV_DIM),
                  blk(rows, HEAD_DIM), blk(rows, HEAD_DIM)],
        out_specs=(blk(DEC_SEQ, ATTN_DIM), blk(rows, HEAD_DIM), blk(rows, HEAD_DIM)),
        scratch_shapes=[pltpu.VMEM((N_HEADS, DEC_SEQ, 2 * BLOCK), F32)],
        compiler_params=_params(("arbitrary",)),
        name="attn_sample",
    )(table, sinks, bucket, q, kn, vn, ck, cv)


def _to_step_major(a):
    return a.transpose(1, 0, 2).reshape(M_SAMPLE, a.shape[-1])


def _to_seq_major(a):
    return a.reshape(DEC_SEQ, DEC_BATCH, a.shape[-1]).transpose(1, 0, 2)


def _whole(a):
    return [(a, 0, M_ALL)]


N_MOD_EARLY = 2


def _ffn(x_parts, mods, k0, g, w1, w3, w2, ada_rest=None, rounders=()):
    h = _norm_mod(x_parts, g, mods[k0], mods[k0 + 1])
    sweep = w1.shape[1] // 256
    if ada_rest is None:
        u, w2_bf16 = _mm_swiglu(
            h, w1, w3, rounders=[(w2, 0, D_MODEL, W2_COLS, 2 * sweep)])
    else:
        n_rest = (N_MOD - N_MOD_EARLY) * D_MODEL
        u, mod_rest, w2_bf16 = _mm_swiglu(
            h, w1, w3, rounders=[(w2, 0, D_MODEL, W2_COLS, 4 * sweep)],
            ada=ada_rest + (N_MOD_EARLY * D_MODEL, n_rest), tm=2304)
        mods += [(mod_rest, k) for k in range(N_MOD - N_MOD_EARLY)]
    return _mm_res([_whole(u)], w2_bf16, x_parts, mods[k0 + 2], 0.5, tm=512, tn=W2_COLS,
                   name="mm_w2", rounders=rounders)


def kernel(x_prompt, x_sample, c_prompt, c_sample, cache_k, cache_v, state_conv, rel_bias,
           g_ffn1, w1_ffn1, w3_ffn1, w2_ffn1, g_mix, w_in, sinks, conv_w, w_out,
           g_ffn2, w1_ffn2, w3_ffn2, w2_ffn2, w_ada, b_ada, g_final):
    depth = w_in.shape[0]
    wb = cache_k.shape[2]
    bucket = jnp.asarray(_bucket_map())
    table = rel_bias.reshape(N_BUCKETS * N_HEADS)
    c_rows = jnp.concatenate(
        [c_sample, c_prompt, jnp.zeros((ADA_ROWS - ROW_GROUP - 1, D_MODEL), F32)], axis=0)
    silu_c = _silu_bf16(c_rows)

    x_parts = [(x_prompt.reshape(M_PROMPT, D_MODEL), 0, M_PROMPT),
               (_to_step_major(x_sample), M_PROMPT, M_SAMPLE)]

    kp, vp, cp, ks, vs, cs = [], [], [], [], [], []
    for l in range(depth):
        b_ada_l = b_ada[l].reshape(1, N_MOD * D_MODEL)
        mod_early = _ada(silu_c, w_ada[l], b_ada_l, N_MOD_EARLY * D_MODEL)
        mods = [(mod_early, k) for k in range(N_MOD_EARLY)]
        x, w_conv = _ffn(
            x_parts, mods, 0, g_ffn1[l], w1_ffn1[l], w3_ffn1[l], w2_ffn1[l],
            ada_rest=(silu_c, w_ada[l], b_ada_l),
            rounders=[(w_in[l], QKV_DIM, 3 * CONV_DIM, CONV_COLS, 128)])

        h = _norm_mod(_whole(x), g_mix[l], mods[3], mods[4])
        qkv, w_out_bf16 = _mm_plain(
            h, w_in[l], QKV_DIM, rounders=[(w_out[l], 0, D_MODEL, 512, 32)])
        yc, tails = _mm_conv(h, w_conv, conv_w[l], state_conv[l].transpose(1, 0, 2))

        o_p = _attn_prompt(qkv, table, sinks[l], bucket)
        qkv_s = qkv[M_PROMPT:]
        q_s = _to_seq_major(qkv_s[:, :ATTN_DIM])
        kn_s = _to_seq_major(qkv_s[:, ATTN_DIM:ATTN_DIM + KV_DIM])
        vn_s = _to_seq_major(qkv_s[:, ATTN_DIM + KV_DIM:])
        o_s, kw_s, vw_s = _attn_sample(
            q_s, kn_s, vn_s,
            cache_k[l].reshape(DEC_BATCH, wb * N_KV_HEADS, HEAD_DIM),
            cache_v[l].reshape(DEC_BATCH, wb * N_KV_HEADS, HEAD_DIM), table, sinks[l], bucket)
        o_parts = [(o_p, 0, M_PROMPT),
                   (o_s.reshape(DEC_BATCH, DEC_SEQ * ATTN_DIM), M_PROMPT, M_SAMPLE, True)]
        x, = _mm_res([o_parts, _whole(yc)], w_out_bf16, _whole(x), mods[5], 1.0, tm=1024,
                     tn=512, name="mm_out")

        x, = _ffn(_whole(x), mods, 6, g_ffn2[l], w1_ffn2[l], w3_ffn2[l], w2_ffn2[l])
        x_parts = _whole(x)

        k_p = qkv[M_PROMPT - wb:M_PROMPT, ATTN_DIM:ATTN_DIM + KV_DIM]
        v_p = qkv[M_PROMPT - wb:M_PROMPT, ATTN_DIM + KV_DIM:]
        kp.append(k_p.reshape(1, wb, N_KV_HEADS, HEAD_DIM))
        vp.append(v_p.reshape(1, wb, N_KV_HEADS, HEAD_DIM))
        last_prompt = M_PROMPT // M_SAMPLE - 1
        cp.append(tails[last_prompt, TAIL_ROWS - (CONV_WIDTH - 1):].reshape(
            1, CONV_WIDTH - 1, CONV_DIM))
        ks.append(kw_s.reshape(DEC_BATCH, wb, N_KV_HEADS, HEAD_DIM))
        vs.append(vw_s.reshape(DEC_BATCH, wb, N_KV_HEADS, HEAD_DIM))
        cs.append(tails[last_prompt + 1].reshape(
            CONV_WIDTH - 1, DEC_BATCH, CONV_DIM).transpose(1, 0, 2))

    y_prompt = _final_norm(x, g_final, 0, M_PROMPT).reshape(1, SEQ, D_MODEL)
    y_sample = _final_norm(x, g_final, M_PROMPT, M_SAMPLE, seq_major_out=True).reshape(
        DEC_BATCH, DEC_SEQ, D_MODEL)
    return (y_prompt, y_sample, jnp.stack(kp), jnp.stack(vp), jnp.stack(cp),
            jnp.stack(ks), jnp.stack(vs), jnp.stack(cs))
```

```python
import functools
import math

import numpy as np
import jax
import jax.numpy as jnp
from jax import lax
from jax.experimental import pallas as pl
from jax.experimental.pallas import tpu as pltpu

F32 = jnp.float32
BF16 = jnp.bfloat16

D_MODEL = 4096
SEQ = 8192
DEC_BATCH = 128
DEC_SEQ = 8
M_PROMPT = SEQ
M_SAMPLE = DEC_BATCH * DEC_SEQ
M_ALL = M_PROMPT + M_SAMPLE
HEAD_DIM = 128
ATTN_DIM = D_MODEL // 2
N_HEADS = ATTN_DIM // HEAD_DIM
N_KV_HEADS = 4
GQA_GROUP = N_HEADS // N_KV_HEADS
KV_DIM = N_KV_HEADS * HEAD_DIM
QKV_DIM = ATTN_DIM + 2 * KV_DIM
CONV_DIM = D_MODEL - ATTN_DIM
CONV_WIDTH = 3
WINDOW = 128
BLOCK = 128
N_BUCKETS = 32
MAX_DISTANCE = 128
D_FF = 11008
N_MOD = 9
PROJ_DIM = QKV_DIM + 3 * CONV_DIM
EPS = 1e-6
NEG = -1e30
SCALE = HEAD_DIM ** -0.5

ROW_GROUP = 128
SUBLANES = 8
NORM_ROWS = 8
VMEM_LIMIT_BYTES = 56 * 1024 * 1024


def _params(semantics):
    return pltpu.CompilerParams(dimension_semantics=semantics,
                                vmem_limit_bytes=VMEM_LIMIT_BYTES)


def _slab_index(row_block, rows_per_block):
    return (row_block < M_PROMPT // rows_per_block).astype(jnp.int32)


def _part_specs(parts, tr, cols, col_index, grid_rank):
    specs = []
    for part in parts:
        row0, rows = part[1], part[2]
        b0, nb = row0 // tr, rows // tr
        blk = lambda i, b0=b0, nb=nb: jnp.clip(i - b0, 0, nb - 1)
        if _is_seq_major(part):
            assert part[0].shape[1] == (rows // DEC_BATCH) * cols
            shape = (DEC_BATCH, (tr // DEC_BATCH) * cols)
            imap = ((lambda i, blk=blk: (0, blk(i))) if grid_rank == 1 else
                    (lambda i, j, blk=blk: (0, blk(i))))
        else:
            shape = (tr, cols)
            imap = ((lambda i, blk=blk: (blk(i), col_index())) if grid_rank == 1 else
                    (lambda i, j, blk=blk: (blk(i), col_index(j))))
        specs.append(pl.BlockSpec(shape, imap))
    return specs


def _is_seq_major(part):
    return len(part) > 3 and part[3]


def _read_part(ref, part, tr):
    v = ref[...]
    if _is_seq_major(part):
        steps = tr // DEC_BATCH
        cols = v.shape[1] // steps
        v = jnp.concatenate([v[:, t * cols:(t + 1) * cols] for t in range(steps)], axis=0)
    return v


def _pick_part(refs, parts, tr, row_block):
    v = _read_part(refs[0], parts[0], tr)
    for ref, part in zip(refs[1:], parts[1:]):
        v = jnp.where(row_block >= part[1] // tr, _read_part(ref, part, tr), v)
    return v


def _bucket_map():
    r = np.arange(BLOCK)[:, None]
    c = np.arange(2 * BLOCK)[None, :]
    n = np.maximum(BLOCK + r - c, 0)
    max_exact = N_BUCKETS // 2
    nf = np.maximum(n, 1).astype(np.float32)
    large = max_exact + (np.log(nf / max_exact) / math.log(MAX_DISTANCE / max_exact)
                         * (N_BUCKETS - max_exact)).astype(np.int32)
    large = np.minimum(large, N_BUCKETS - 1)
    return np.where(n < max_exact, n, large).astype(np.int32)


ADA_ROWS = ROW_GROUP + 16


def _ada_block(a_ref, w_ref, b_ref):
    r = jnp.dot(a_ref[...], w_ref[...].astype(BF16), preferred_element_type=F32) + b_ref[...]
    prompt = jnp.broadcast_to(r[ROW_GROUP:ROW_GROUP + 1], (ROW_GROUP, r.shape[1]))
    return jnp.concatenate([r[:ROW_GROUP], prompt], axis=0)


def _ada_kernel(c_ref, w_ref, b_ref, o_ref, a_ref):
    @pl.when(pl.program_id(0) == 0)
    def _():
        c = c_ref[...]
        a_ref[...] = (c / (1.0 + jnp.exp(-c))).astype(BF16)

    o_ref[...] = _ada_block(a_ref, w_ref, b_ref)


def _ada(c_rows, w_ada, b_ada, n, tn=512):
    return pl.pallas_call(
        _ada_kernel,
        out_shape=(jax.ShapeDtypeStruct((2 * ROW_GROUP, n), F32),
                   jax.ShapeDtypeStruct((ADA_ROWS, D_MODEL), BF16)),
        grid=(n // tn,),
        in_specs=[pl.BlockSpec((ADA_ROWS, D_MODEL), lambda j: (0, 0)),
                  pl.BlockSpec((D_MODEL, tn), lambda j: (0, j)),
                  pl.BlockSpec((1, tn), lambda j: (0, j))],
        out_specs=(pl.BlockSpec((2 * ROW_GROUP, tn), lambda j: (0, j)),
                   pl.BlockSpec((ADA_ROWS, D_MODEL), lambda j: (0, 0))),
        compiler_params=_params(("arbitrary",)),
        name="ada",
    )(c_rows, w_ada, b_ada)


def _norm_mod_kernel(*refs, parts, tr):
    n = len(parts)
    g_ref, sh_ref, sc_ref, o_ref = refs[n:]
    i = pl.program_id(0)
    for r0 in range(0, tr, NORM_ROWS):
        rows = slice(r0, r0 + NORM_ROWS)
        slab_rows = slice(r0 % ROW_GROUP, r0 % ROW_GROUP + NORM_ROWS)
        x = refs[0][rows, :]
        for ref, part in zip(refs[1:n], parts[1:]):
            x = jnp.where(i >= part[1] // tr, ref[rows, :], x)
        y = x * lax.rsqrt(jnp.mean(x * x, axis=-1, keepdims=True) + EPS) * g_ref[...]
        o_ref[rows, :] = (y * (1.0 + sc_ref[slab_rows, :]) + sh_ref[slab_rows, :]).astype(BF16)


def _norm_mod(parts, g, shift, scale):
    tr = 512 // len(parts)
    slab = lambda v: pl.BlockSpec((ROW_GROUP, D_MODEL), lambda i: (_slab_index(i, tr), v[1]))
    return pl.pallas_call(
        functools.partial(_norm_mod_kernel, parts=parts, tr=tr),
        out_shape=jax.ShapeDtypeStruct((M_ALL, D_MODEL), BF16),
        grid=(M_ALL // tr,),
        in_specs=_part_specs(parts, tr, D_MODEL, lambda: 0, 1) + [
            pl.BlockSpec((1, D_MODEL), lambda i: (0, 0)), slab(shift), slab(scale)],
        out_specs=pl.BlockSpec((tr, D_MODEL), lambda i: (i, 0)),
        compiler_params=_params(("parallel",)),
        name="norm_mod",
    )(*[p[0] for p in parts], g.reshape(1, D_MODEL), shift[0], scale[0])


def _final_norm_kernel(x_ref, g_ref, o_ref):
    x = x_ref[...]
    o_ref[...] = x * lax.rsqrt(jnp.mean(x * x, axis=-1, keepdims=True) + EPS) * g_ref[...]


def _final_norm(x, g, row0, rows, tr=512, seq_major_out=False):
    if seq_major_out:
        tr = DEC_BATCH
        out_shape = jax.ShapeDtypeStruct((DEC_BATCH, (rows // DEC_BATCH) * D_MODEL), F32)
        out_spec = pl.BlockSpec((tr, D_MODEL), lambda i: (0, i))
    else:
        out_shape = jax.ShapeDtypeStruct((rows, D_MODEL), F32)
        out_spec = pl.BlockSpec((tr, D_MODEL), lambda i: (i, 0))
    return pl.pallas_call(
        _final_norm_kernel,
        out_shape=out_shape,
        grid=(rows // tr,),
        in_specs=[pl.BlockSpec((tr, D_MODEL), lambda i: (i + row0 // tr, 0)),
                  pl.BlockSpec((1, D_MODEL), lambda i: (0, 0))],
        out_specs=out_spec,
        compiler_params=_params(("parallel",)),
        name="final_norm",
    )(x, g.reshape(1, D_MODEL))


def _resident_rows(tm, k, index_map):
    return pl.BlockSpec((tm, k), index_map, pipeline_mode=pl.Buffered(1))


def _grid_step():
    return pl.program_id(0) * pl.num_programs(1) + pl.program_id(1)


def _rounder_specs(rounders, n_inner, total_steps):
    in_specs, out_shapes, out_specs, static = [], [], [], []
    for src, col0, n_cols, cols, steps in rounders:
        assert total_steps >= steps and src.shape[0] % steps == 0 and n_cols % cols == 0
        rows = src.shape[0] // steps
        blk = lambda i, j, steps=steps: jnp.minimum(i * n_inner + j, steps - 1)
        in_specs.append(pl.BlockSpec((rows, src.shape[1]), lambda i, j, blk=blk: (blk(i, j), 0)))
        out_shapes.append(jax.ShapeDtypeStruct((n_cols // cols, src.shape[0], cols), BF16))
        out_specs.append(pl.BlockSpec((n_cols // cols, rows, cols),
                                      lambda i, j, blk=blk: (0, blk(i, j), 0)))
        static.append((col0, steps))
    return in_specs, out_shapes, out_specs, tuple(static)


def _run_rounders(static, src_refs, dst_refs):
    step = _grid_step()
    for (col0, steps), src_ref, dst_ref in zip(static, src_refs, dst_refs):
        @pl.when(step < steps)
        def _(col0=col0, src_ref=src_ref, dst_ref=dst_ref):
            cols = dst_ref.shape[2]
            for nb in range(dst_ref.shape[0]):
                lo = col0 + nb * cols
                dst_ref[nb] = src_ref[:, lo:lo + cols].astype(BF16)


def _mm_plain_kernel(*refs, rounders):
    n_r = len(rounders)
    x_ref, w_ref = refs[:2]
    o_ref = refs[2 + n_r]
    o_ref[...] = jnp.dot(x_ref[...], w_ref[...].astype(BF16), preferred_element_type=F32)
    _run_rounders(rounders, refs[2:2 + n_r], refs[3 + n_r:])


def _mm_plain(x, w, n, rounders=(), tm=1536, tn=512):
    m, k = x.shape
    nj = n // tn
    r_in, r_shapes, r_out, r_static = _rounder_specs(rounders, nj, (m // tm) * nj)
    return pl.pallas_call(
        functools.partial(_mm_plain_kernel, rounders=r_static),
        out_shape=(jax.ShapeDtypeStruct((m, n), F32), *r_shapes),
        grid=(m // tm, nj),
        in_specs=[pl.BlockSpec((tm, k), lambda i, j: (i, 0)),
                  pl.BlockSpec((k, tn), lambda i, j: (0, j))] + r_in,
        out_specs=(pl.BlockSpec((tm, tn), lambda i, j: (i, j)), *r_out),
        compiler_params=_params(("arbitrary", "arbitrary")),
        name="mm_qkv",
    )(x, w, *[r[0] for r in rounders])


def _mm_swiglu_kernel(*refs, rounders, ada_steps):
    n_r = len(rounders)
    n_ada = 3 if ada_steps else 0
    x_ref, w1_ref, w3_ref = refs[:3]
    n_in = 3 + n_ada + n_r
    o_ref = refs[n_in]
    x = x_ref[...]
    a = jnp.dot(x, w1_ref[...].astype(BF16), preferred_element_type=F32)
    b = jnp.dot(x, w3_ref[...].astype(BF16), preferred_element_type=F32)
    o_ref[...] = (a / (1.0 + jnp.exp(-a)) * b).astype(BF16)
    _run_rounders(rounders, refs[3 + n_ada:n_in], refs[len(refs) - n_r:])

    if ada_steps:
        @pl.when(_grid_step() < ada_steps)
        def _():
            refs[n_in + 1][...] = _ada_block(*refs[3:6])


W2_COLS = 512
ADA_COLS = 256


def _mm_swiglu(x, w1, w3, rounders=(), ada=None, tm=3072, tn=256):
    m, k = x.shape
    n = w1.shape[1]
    nj = n // tn
    steps = (m // tm) * nj
    in_specs = [_resident_rows(tm, k, lambda i, j: (i, 0)),
                pl.BlockSpec((k, tn), lambda i, j: (0, j)),
                pl.BlockSpec((k, tn), lambda i, j: (0, j))]
    out_shape = [jax.ShapeDtypeStruct((m, n), BF16)]
    out_specs = [pl.BlockSpec((tm, tn), lambda i, j: (i, j))]
    args = [x, w1, w3]
    ada_steps = 0
    if ada is not None:
        a, w_ada, b_ada, col0, n_ada = ada
        ada_steps = n_ada // ADA_COLS
        assert steps >= ada_steps
        ada_blk = lambda i, j: jnp.minimum(i * nj + j, ada_steps - 1)
        in_specs += [
            pl.BlockSpec(a.shape, lambda i, j: (0, 0), pipeline_mode=pl.Buffered(1)),
            pl.BlockSpec((k, ADA_COLS), lambda i, j: (0, col0 // ADA_COLS + ada_blk(i, j))),
            pl.BlockSpec((1, ADA_COLS), lambda i, j: (0, col0 // ADA_COLS + ada_blk(i, j)))]
        out_shape.append(jax.ShapeDtypeStruct((2 * ROW_GROUP, n_ada), F32))
        out_specs.append(pl.BlockSpec((2 * ROW_GROUP, ADA_COLS), lambda i, j: (0, ada_blk(i, j))))
        args += [a, w_ada, b_ada]
    r_in, r_shapes, r_out, r_static = _rounder_specs(rounders, nj, steps)
    return pl.pallas_call(
        functools.partial(_mm_swiglu_kernel, rounders=r_static, ada_steps=ada_steps),
        out_shape=(*out_shape, *r_shapes),
        grid=(m // tm, nj),
        in_specs=in_specs + r_in,
        out_specs=(*out_specs, *r_out),
        compiler_params=_params(("arbitrary", "arbitrary")),
        name="mm_swiglu",
    )(*args, *[r[0] for r in rounders])


def _mm_res_kernel(*refs, x_parts, res_parts, tm, coef, rounders):
    i = pl.program_id(0)
    pos = 0
    xs = []
    for parts in x_parts:
        xs.append(_pick_part(refs[pos:pos + len(parts)], parts, tm, i))
        pos += len(parts)
    w_ref = refs[pos]
    res = _pick_part(refs[pos + 1:pos + 1 + len(res_parts)], res_parts, tm, i)
    gate_ref = refs[pos + 1 + len(res_parts)]
    n_in = pos + 2 + len(res_parts) + len(rounders)
    o_ref = refs[n_in]
    x = xs[0] if len(xs) == 1 else jnp.concatenate(xs, axis=1)
    acc = jnp.dot(x, w_ref[...].astype(BF16), preferred_element_type=F32)
    tn = acc.shape[1]
    gate = coef * gate_ref[...]
    y = acc.reshape(tm // ROW_GROUP, ROW_GROUP, tn) * gate[None]
    o_ref[...] = res + y.reshape(tm, tn)
    _run_rounders(rounders, refs[n_in - len(rounders):n_in], refs[n_in + 1:])


def _mm_res(x_parts, w, res_parts, gate, coef, tm, tn, name, rounders=()):
    mod, k_gate = gate
    if w.ndim == 3:
        assert w.shape[2] == tn
        k, n = w.shape[1], w.shape[0] * tn
        w_spec = pl.BlockSpec((None, k, tn), lambda i, j: (j, 0, 0))
    else:
        k, n = w.shape
        w_spec = pl.BlockSpec((k, tn), lambda i, j: (0, j))
    gate_blocks = D_MODEL // tn
    nj = n // tn
    in_specs, args = [], []
    for parts in x_parts:
        in_specs += _part_specs(parts, tm, parts[0][0].shape[1], lambda j: 0, 2)
        args += [p[0] for p in parts]
    in_specs.append(w_spec)
    in_specs += _part_specs(res_parts, tm, tn, lambda j: j, 2)
    in_specs.append(pl.BlockSpec(
        (ROW_GROUP, tn), lambda i, j: (_slab_index(i, tm), k_gate * gate_blocks + j)))
    args += [w] + [p[0] for p in res_parts] + [mod]
    r_in, r_shapes, r_out, r_static = _rounder_specs(rounders, nj, (M_ALL // tm) * nj)
    return pl.pallas_call(
        functools.partial(_mm_res_kernel, x_parts=x_parts, res_parts=res_parts, tm=tm,
                          coef=coef, rounders=r_static),
        out_shape=(jax.ShapeDtypeStruct((M_ALL, n), F32), *r_shapes),
        grid=(M_ALL // tm, nj),
        in_specs=in_specs + r_in,
        out_specs=(pl.BlockSpec((tm, tn), lambda i, j: (i, j)), *r_out),
        compiler_params=_params(("arbitrary", "arbitrary")),
        name=name,
    )(*args, *[r[0] for r in rounders])


CONV_COLS = 256
TAIL_ROWS = (CONV_WIDTH - 1) * ROW_GROUP


def _mm_conv_kernel(x_ref, wb_ref, wc_ref, wh_ref, cw_ref, st_ref, yc_ref, tail_ref,
                    shift_scr, carry_scr):
    i = pl.program_id(0)
    c = pl.program_id(1)
    tm = x_ref.shape[0]
    x = x_ref[...]
    dot = lambda w_ref: jnp.dot(x, w_ref[...].astype(BF16), preferred_element_type=F32)
    gate_b = dot(wb_ref)
    u = dot(wc_ref) * dot(wh_ref)
    w0, w1, w2 = cw_ref[0:1, :], cw_ref[1:2, :], cw_ref[2:3, :]
    tail_ref[0] = u[tm - TAIL_ROWS:, :]
    n_prompt_blocks = M_PROMPT // tm

    @pl.when(i < n_prompt_blocks)
    def _():
        @pl.when(i == 0)
        def _():
            shift_scr[0:SUBLANES, :] = jnp.zeros((SUBLANES, CONV_COLS), F32)

        @pl.when(i > 0)
        def _():
            shift_scr[0:SUBLANES, :] = carry_scr[c]

        shift_scr[SUBLANES:SUBLANES + tm, :] = u
        carry_scr[c] = u[tm - SUBLANES:, :]
        u1 = shift_scr[SUBLANES - 1:SUBLANES - 1 + tm, :]
        u2 = shift_scr[SUBLANES - 2:SUBLANES - 2 + tm, :]
        yc_ref[...] = (gate_b * (u2 * w0 + u1 * w1 + u * w2)).astype(BF16)

    @pl.when(i >= n_prompt_blocks)
    def _():
        shape = (DEC_SEQ, DEC_BATCH, CONV_COLS)
        u3 = u.reshape(shape)
        st = st_ref[...]
        u1 = jnp.concatenate([st[1:2], u3[:DEC_SEQ - 1]], axis=0)
        u2 = jnp.concatenate([st[0:2], u3[:DEC_SEQ - 2]], axis=0)
        y = u2 * w0[None] + u1 * w1[None] + u3 * w2[None]
        yc_ref[...] = (gate_b.reshape(shape) * y).reshape(tm, CONV_COLS).astype(BF16)


def _mm_conv(x, w_in, conv_w, state, tm=M_SAMPLE):
    m, k = x.shape
    nc = CONV_DIM // CONV_COLS
    first = QKV_DIM // CONV_COLS
    wcol = lambda base: pl.BlockSpec((k, CONV_COLS), lambda i, c: (0, first + base + c))
    return pl.pallas_call(
        _mm_conv_kernel,
        out_shape=(jax.ShapeDtypeStruct((m, CONV_DIM), BF16),
                   jax.ShapeDtypeStruct((m // tm, TAIL_ROWS, CONV_DIM), F32)),
        grid=(m // tm, nc),
        in_specs=[pl.BlockSpec((tm, k), lambda i, c: (i, 0)),
                  wcol(0), wcol(nc), wcol(2 * nc),
                  pl.BlockSpec((CONV_WIDTH, CONV_COLS), lambda i, c: (0, c)),
                  pl.BlockSpec((CONV_WIDTH - 1, DEC_BATCH, CONV_COLS), lambda i, c: (0, 0, c))],
        out_specs=(pl.BlockSpec((tm, CONV_COLS), lambda i, c: (i, c)),
                   pl.BlockSpec((1, TAIL_ROWS, CONV_COLS), lambda i, c: (i, 0, c))),
        scratch_shapes=[pltpu.VMEM((tm + SUBLANES, CONV_COLS), F32),
                        pltpu.VMEM((nc, SUBLANES, CONV_COLS), F32)],
        compiler_params=_params(("arbitrary", "arbitrary")),
        name="mm_conv",
    )(x, w_in, w_in, w_in, conv_w, state)


def _build_bias(tab_ref, bucket, bias_scr):
    for h in range(N_HEADS):
        acc = jnp.zeros(bucket.shape, F32)
        for b in range(N_BUCKETS):
            acc = jnp.where(bucket == b, tab_ref[b * N_HEADS + h], acc)
        bias_scr[h] = acc


def _valid_mask(rows, has_prev):
    r = lax.broadcasted_iota(jnp.int32, (rows, 2 * BLOCK), 0)
    c = lax.broadcasted_iota(jnp.int32, (rows, 2 * BLOCK), 1)
    dist = BLOCK + r - c
    return (dist >= 0) & (dist <= WINDOW) & ((c >= BLOCK) | has_prev)


def _softmax_with_sink(s, sink):
    m = jnp.maximum(jnp.max(s, axis=-1, keepdims=True), sink)
    p = jnp.exp(s - m)
    denom = jnp.sum(p, axis=-1, keepdims=True) + jnp.exp(sink - m)
    return p / denom


QUERY_BLOCKS = 2


def _attn_prompt_kernel(tab_ref, sink_ref, bucket_ref, q_ref, kc_ref, kp_ref, vc_ref, vp_ref,
                        o_ref, bias_scr):
    i = pl.program_id(0)

    @pl.when(i == 0)
    def _():
        _build_bias(tab_ref, bucket_ref[...], bias_scr)

    for g in range(N_KV_HEADS):
        kv = slice(g * HEAD_DIM, (g + 1) * HEAD_DIM)
        heads = [g * GQA_GROUP + j for j in range(GQA_GROUP)]
        for b in range(QUERY_BLOCKS):
            rows = slice(b * BLOCK, (b + 1) * BLOCK)
            if b == 0:
                k_prev, v_prev = kp_ref[:, kv], vp_ref[:, kv]
                valid = _valid_mask(BLOCK, i > 0)
            else:
                before = slice((b - 1) * BLOCK, b * BLOCK)
                k_prev, v_prev = kc_ref[before, kv], vc_ref[before, kv]
                valid = _valid_mask(BLOCK, True)
            qs = jnp.concatenate(
                [q_ref[rows, h * HEAD_DIM:(h + 1) * HEAD_DIM] for h in heads], axis=0).astype(BF16)
            kk = jnp.concatenate([k_prev, kc_ref[rows, kv]], axis=0).astype(BF16)
            vv = jnp.concatenate([v_prev, vc_ref[rows, kv]], axis=0).astype(BF16)
            s = lax.dot_general(qs, kk, (((1,), (1,)), ((), ())), preferred_element_type=F32)
            ps = []
            for j, h in enumerate(heads):
                sj = s[j * BLOCK:(j + 1) * BLOCK] * SCALE + bias_scr[h]
                sj = jnp.where(valid, sj, NEG)
                ps.append(_softmax_with_sink(sj, sink_ref[h]))
            p = jnp.concatenate(ps, axis=0).astype(BF16)
            o = jnp.dot(p, vv, preferred_element_type=F32)
            for j, h in enumerate(heads):
                o_ref[rows, h * HEAD_DIM:(h + 1) * HEAD_DIM] = (
                    o[j * BLOCK:(j + 1) * BLOCK].astype(BF16))


def _attn_prompt(qkv, table, sinks, bucket):
    rows = QUERY_BLOCKS * BLOCK
    q_blk = 0
    k_blk = ATTN_DIM // KV_DIM
    v_blk = k_blk + 1
    prev = lambda i: jnp.maximum(QUERY_BLOCKS * i - 1, 0)
    smem = pl.BlockSpec(memory_space=pltpu.SMEM)
    return pl.pallas_call(
        _attn_prompt_kernel,
        out_shape=jax.ShapeDtypeStruct((M_PROMPT, ATTN_DIM), BF16),
        grid=(M_PROMPT // rows,),
        in_specs=[smem, smem,
                  pl.BlockSpec((BLOCK, 2 * BLOCK), lambda i: (0, 0)),
                  pl.BlockSpec((rows, ATTN_DIM), lambda i: (i, q_blk)),
                  pl.BlockSpec((rows, KV_DIM), lambda i: (i, k_blk)),
                  pl.BlockSpec((BLOCK, KV_DIM), lambda i: (prev(i), k_blk)),
                  pl.BlockSpec((rows, KV_DIM), lambda i: (i, v_blk)),
                  pl.BlockSpec((BLOCK, KV_DIM), lambda i: (prev(i), v_blk))],
        out_specs=pl.BlockSpec((rows, ATTN_DIM), lambda i: (i, 0)),
        scratch_shapes=[pltpu.VMEM((N_HEADS, BLOCK, 2 * BLOCK), F32)],
        compiler_params=_params(("arbitrary",)),
        name="attn_prompt",
    )(table, sinks, bucket, qkv, qkv, qkv, qkv, qkv)


def _attn_sample_kernel(tab_ref, sink_ref, bucket_ref, q_ref, kn_ref, vn_ref, ck_ref, cv_ref,
                        o_ref, kw_ref, vw_ref, bias_scr):
    i = pl.program_id(0)
    g_seq = q_ref.shape[0]
    wb = ck_ref.shape[1] // N_KV_HEADS

    @pl.when(i == 0)
    def _():
        _build_bias(tab_ref, bucket_ref[0:DEC_SEQ, :], bias_scr)

    valid = _valid_mask(DEC_SEQ, True)[None]
    pad = jnp.zeros((g_seq, 2 * BLOCK - wb - DEC_SEQ, HEAD_DIM), F32)
    for g in range(N_KV_HEADS):
        kv = slice(g * HEAD_DIM, (g + 1) * HEAD_DIM)
        head_rows = pl.ds(g, wb, stride=N_KV_HEADS)
        heads = [g * GQA_GROUP + j for j in range(GQA_GROUP)]
        qs = jnp.concatenate(
            [q_ref[:, :, h * HEAD_DIM:(h + 1) * HEAD_DIM] for h in heads], axis=1).astype(BF16)
        kk = jnp.concatenate([ck_ref[:, head_rows, :], kn_ref[:, :, kv], pad], axis=1).astype(BF16)
        vv = jnp.concatenate([cv_ref[:, head_rows, :], vn_ref[:, :, kv], pad], axis=1).astype(BF16)
        s = jnp.einsum('gqd,gkd->gqk', qs, kk, preferred_element_type=F32)
        ps = []
        for j, h in enumerate(heads):
            sj = s[:, j * DEC_SEQ:(j + 1) * DEC_SEQ] * SCALE + bias_scr[h][None]
            sj = jnp.where(valid, sj, NEG)
            ps.append(_softmax_with_sink(sj, sink_ref[h]))
        p = jnp.concatenate(ps, axis=1).astype(BF16)
        o = jnp.einsum('gqk,gkd->gqd', p, vv, preferred_element_type=F32)
        for j, h in enumerate(heads):
            o_ref[:, :, h * HEAD_DIM:(h + 1) * HEAD_DIM] = (
                o[:, j * DEC_SEQ:(j + 1) * DEC_SEQ].astype(BF16))
        new_rows = pl.ds((wb - DEC_SEQ) * N_KV_HEADS + g, DEC_SEQ, stride=N_KV_HEADS)
        kw_ref[:, new_rows, :] = kn_ref[:, :, kv]
        vw_ref[:, new_rows, :] = vn_ref[:, :, kv]
    kept = (wb - DEC_SEQ) * N_KV_HEADS
    kw_ref[:, 0:kept, :] = ck_ref[:, DEC_SEQ * N_KV_HEADS:, :]
    vw_ref[:, 0:kept, :] = cv_ref[:, DEC_SEQ * N_KV_HEADS:, :]


def _attn_sample(q, kn, vn, ck, cv, table, sinks, bucket, g_seq=16):
    rows = ck.shape[1]
    smem = pl.BlockSpec(memory_space=pltpu.SMEM)
    blk = lambda r, c: pl.BlockSpec((g_seq, r, c), lambda i: (i, 0, 0))
    return pl.pallas_call(
        _attn_sample_kernel,
        out_shape=(jax.ShapeDtypeStruct((DEC_BATCH, DEC_SEQ, ATTN_DIM), BF16),
                   jax.ShapeDtypeStruct((DEC_BATCH, rows, HEAD_DIM), F32),
                   jax.ShapeDtypeStruct((DEC_BATCH, rows, HEAD_DIM), F32)),
        grid=(DEC_BATCH // g_seq,),
        in_specs=[smem, smem,
                  pl.BlockSpec((BLOCK, 2 * BLOCK), lambda i: (0, 0)),
                  blk(DEC_SEQ, ATTN_DIM), blk(DEC_SEQ, KV_DIM), blk(DEC_SEQ, KV_DIM),
                  blk(rows, HEAD_DIM), blk(rows, HEAD_DIM)],
        out_specs=(blk(DEC_SEQ, ATTN_DIM), blk(rows, HEAD_DIM), blk(rows, HEAD_DIM)),
        scratch_shapes=[pltpu.VMEM((N_HEADS, DEC_SEQ, 2 * BLOCK), F32)],
        compiler_params=_params(("arbitrary",)),
        name="attn_sample",
    )(table, sinks, bucket, q, kn, vn, ck, cv)


def _to_step_major(a):
    return a.transpose(1, 0, 2).reshape(M_SAMPLE, a.shape[-1])


def _to_seq_major(a):
    return a.reshape(DEC_SEQ, DEC_BATCH, a.shape[-1]).transpose(1, 0, 2)


def _whole(a):
    return [(a, 0, M_ALL)]


N_MOD_EARLY = 2


def _ffn(x_parts, mods, k0, g, w1, w3, w2, ada_rest=None, rounders=()):
    h = _norm_mod(x_parts, g, mods[k0], mods[k0 + 1])
    sweep = w1.shape[1] // 256
    if ada_rest is None:
        u, w2_bf16 = _mm_swiglu(
            h, w1, w3, rounders=[(w2, 0, D_MODEL, W2_COLS, 2 * sweep)])
    else:
        n_rest = (N_MOD - N_MOD_EARLY) * D_MODEL
        u, mod_rest, w2_bf16 = _mm_swiglu(
            h, w1, w3, rounders=[(w2, 0, D_MODEL, W2_COLS, 4 * sweep)],
            ada=ada_rest + (N_MOD_EARLY * D_MODEL, n_rest), tm=2304)
        mods += [(mod_rest, k) for k in range(N_MOD - N_MOD_EARLY)]
    return _mm_res([_whole(u)], w2_bf16, x_parts, mods[k0 + 2], 0.5, tm=512, tn=W2_COLS,
                   name="mm_w2", rounders=rounders)


def kernel(x_prompt, x_sample, c_prompt, c_sample, cache_k, cache_v, state_conv, rel_bias,
           g_ffn1, w1_ffn1, w3_ffn1, w2_ffn1, g_mix, w_in, sinks, conv_w, w_out,
           g_ffn2, w1_ffn2, w3_ffn2, w2_ffn2, w_ada, b_ada, g_final):
    depth = w_in.shape[0]
    wb = cache_k.shape[2]
    bucket = jnp.asarray(_bucket_map())
    table = rel_bias.reshape(N_BUCKETS * N_HEADS)
    c_rows = jnp.concatenate(
        [c_sample, c_prompt, jnp.zeros((ADA_ROWS - ROW_GROUP - 1, D_MODEL), F32)], axis=0)

    x_parts = [(x_prompt.reshape(M_PROMPT, D_MODEL), 0, M_PROMPT),
               (_to_step_major(x_sample), M_PROMPT, M_SAMPLE)]

    kp, vp, cp, ks, vs, cs = [], [], [], [], [], []
    for l in range(depth):
        b_ada_l = b_ada[l].reshape(1, N_MOD * D_MODEL)
        mod_early, silu_c = _ada(c_rows, w_ada[l], b_ada_l, N_MOD_EARLY * D_MODEL)
        mods = [(mod_early, k) for k in range(N_MOD_EARLY)]
        x, = _ffn(x_parts, mods, 0, g_ffn1[l], w1_ffn1[l], w3_ffn1[l], w2_ffn1[l],
                  ada_rest=(silu_c, w_ada[l], b_ada_l))

        h = _norm_mod(_whole(x), g_mix[l], mods[3], mods[4])
        qkv, = _mm_plain(h, w_in[l], QKV_DIM)
        yc, tails = _mm_conv(h, w_in[l], conv_w[l], state_conv[l].transpose(1, 0, 2))

        o_p = _attn_prompt(qkv, table, sinks[l], bucket)
        qkv_s = qkv[M_PROMPT:]
        q_s = _to_seq_major(qkv_s[:, :ATTN_DIM])
        kn_s = _to_seq_major(qkv_s[:, ATTN_DIM:ATTN_DIM + KV_DIM])
        vn_s = _to_seq_major(qkv_s[:, ATTN_DIM + KV_DIM:])
        o_s, kw_s, vw_s = _attn_sample(
            q_s, kn_s, vn_s,
            cache_k[l].reshape(DEC_BATCH, wb * N_KV_HEADS, HEAD_DIM),
            cache_v[l].reshape(DEC_BATCH, wb * N_KV_HEADS, HEAD_DIM), table, sinks[l], bucket)
        o_parts = [(o_p, 0, M_PROMPT),
                   (o_s.reshape(DEC_BATCH, DEC_SEQ * ATTN_DIM), M_PROMPT, M_SAMPLE, True)]
        x, = _mm_res([o_parts, _whole(yc)], w_out[l], _whole(x), mods[5], 1.0, tm=1024,
                     tn=512, name="mm_out")

        x, = _ffn(_whole(x), mods, 6, g_ffn2[l], w1_ffn2[l], w3_ffn2[l], w2_ffn2[l])
        x_parts = _whole(x)

        k_p = qkv[M_PROMPT - wb:M_PROMPT, ATTN_DIM:ATTN_DIM + KV_DIM]
        v_p = qkv[M_PROMPT - wb:M_PROMPT, ATTN_DIM + KV_DIM:]
        kp.append(k_p.reshape(1, wb, N_KV_HEADS, HEAD_DIM))
        vp.append(v_p.reshape(1, wb, N_KV_HEADS, HEAD_DIM))
        last_prompt = M_PROMPT // M_SAMPLE - 1
        cp.append(tails[last_prompt, TAIL_ROWS - (CONV_WIDTH - 1):].reshape(
            1, CONV_WIDTH - 1, CONV_DIM))
        ks.append(kw_s.reshape(DEC_BATCH, wb, N_KV_HEADS, HEAD_DIM))
        vs.append(vw_s.reshape(DEC_BATCH, wb, N_KV_HEADS, HEAD_DIM))
        cs.append(tails[last_prompt + 1].reshape(
            CONV_WIDTH - 1, DEC_BATCH, CONV_DIM).transpose(1, 0, 2))

    y_prompt = _final_norm(x, g_final, 0, M_PROMPT).reshape(1, SEQ, D_MODEL)
    y_sample = _final_norm(x, g_final, M_PROMPT, M_SAMPLE, seq_major_out=True).reshape(
        DEC_BATCH, DEC_SEQ, D_MODEL)
    return (y_prompt, y_sample, jnp.stack(kp), jnp.stack(vp), jnp.stack(cp),
            jnp.stack(ks), jnp.stack(vs), jnp.stack(cs))
```

```python
import functools
import math

import numpy as np
import jax
import jax.numpy as jnp
from jax import lax
from jax.experimental import pallas as pl
from jax.experimental.pallas import tpu as pltpu

F32 = jnp.float32
BF16 = jnp.bfloat16

D_MODEL = 4096
SEQ = 8192
DEC_BATCH = 128
DEC_SEQ = 8
M_PROMPT = SEQ
M_SAMPLE = DEC_BATCH * DEC_SEQ
M_ALL = M_PROMPT + M_SAMPLE
HEAD_DIM = 128
ATTN_DIM = D_MODEL // 2
N_HEADS = ATTN_DIM // HEAD_DIM
N_KV_HEADS = 4
GQA_GROUP = N_HEADS // N_KV_HEADS
KV_DIM = N_KV_HEADS * HEAD_DIM
QKV_DIM = ATTN_DIM + 2 * KV_DIM
CONV_DIM = D_MODEL - ATTN_DIM
CONV_WIDTH = 3
WINDOW = 128
BLOCK = 128
N_BUCKETS = 32
MAX_DISTANCE = 128
D_FF = 11008
N_MOD = 9
PROJ_DIM = QKV_DIM + 3 * CONV_DIM
EPS = 1e-6
NEG = -1e30
SCALE = HEAD_DIM ** -0.5

ROW_GROUP = 128
SUBLANES = 8
NORM_ROWS = 8
VMEM_LIMIT_BYTES = 56 * 1024 * 1024


def _params(semantics):
    return pltpu.CompilerParams(dimension_semantics=semantics,
                                vmem_limit_bytes=VMEM_LIMIT_BYTES)


def _slab_index(row_block, rows_per_block):
    return (row_block < M_PROMPT // rows_per_block).astype(jnp.int32)


def _part_specs(parts, tr, cols, col_index, grid_rank):
    specs = []
    for part in parts:
        row0, rows = part[1], part[2]
        b0, nb = row0 // tr, rows // tr
        blk = lambda i, b0=b0, nb=nb: jnp.clip(i - b0, 0, nb - 1)
        if _is_seq_major(part):
            assert part[0].shape[1] == (rows // DEC_BATCH) * cols
            shape = (DEC_BATCH, (tr // DEC_BATCH) * cols)
            imap = ((lambda i, blk=blk: (0, blk(i))) if grid_rank == 1 else
                    (lambda i, j, blk=blk: (0, blk(i))))
        else:
            shape = (tr, cols)
            imap = ((lambda i, blk=blk: (blk(i), col_index())) if grid_rank == 1 else
                    (lambda i, j, blk=blk: (blk(i), col_index(j))))
        specs.append(pl.BlockSpec(shape, imap))
    return specs


def _is_seq_major(part):
    return len(part) > 3 and part[3]


def _read_part(ref, part, tr):
    v = ref[...]
    if _is_seq_major(part):
        steps = tr // DEC_BATCH
        cols = v.shape[1] // steps
        v = jnp.concatenate([v[:, t * cols:(t + 1) * cols] for t in range(steps)], axis=0)
    return v


def _pick_part(refs, parts, tr, row_block):
    v = _read_part(refs[0], parts[0], tr)
    for ref, part in zip(refs[1:], parts[1:]):
        v = jnp.where(row_block >= part[1] // tr, _read_part(ref, part, tr), v)
    return v


def _bucket_map():
    r = np.arange(BLOCK)[:, None]
    c = np.arange(2 * BLOCK)[None, :]
    n = np.maximum(BLOCK + r - c, 0)
    max_exact = N_BUCKETS // 2
    nf = np.maximum(n, 1).astype(np.float32)
    large = max_exact + (np.log(nf / max_exact) / math.log(MAX_DISTANCE / max_exact)
                         * (N_BUCKETS - max_exact)).astype(np.int32)
    large = np.minimum(large, N_BUCKETS - 1)
    return np.where(n < max_exact, n, large).astype(np.int32)


ADA_ROWS = ROW_GROUP + 16


def _ada_block(a_ref, w_ref, b_ref):
    r = jnp.dot(a_ref[...], w_ref[...].astype(BF16), preferred_element_type=F32) + b_ref[...]
    prompt = jnp.broadcast_to(r[ROW_GROUP:ROW_GROUP + 1], (ROW_GROUP, r.shape[1]))
    return jnp.concatenate([r[:ROW_GROUP], prompt], axis=0)


def _ada_kernel(c_ref, w_ref, b_ref, o_ref, a_ref):
    @pl.when(pl.program_id(0) == 0)
    def _():
        c = c_ref[...]
        a_ref[...] = (c / (1.0 + jnp.exp(-c))).astype(BF16)

    o_ref[...] = _ada_block(a_ref, w_ref, b_ref)


def _ada(c_rows, w_ada, b_ada, n, tn=512):
    return pl.pallas_call(
        _ada_kernel,
        out_shape=(jax.ShapeDtypeStruct((2 * ROW_GROUP, n), F32),
                   jax.ShapeDtypeStruct((ADA_ROWS, D_MODEL), BF16)),
        grid=(n // tn,),
        in_specs=[pl.BlockSpec((ADA_ROWS, D_MODEL), lambda j: (0, 0)),
                  pl.BlockSpec((D_MODEL, tn), lambda j: (0, j)),
                  pl.BlockSpec((1, tn), lambda j: (0, j))],
        out_specs=(pl.BlockSpec((2 * ROW_GROUP, tn), lambda j: (0, j)),
                   pl.BlockSpec((ADA_ROWS, D_MODEL), lambda j: (0, 0))),
        compiler_params=_params(("arbitrary",)),
        name="ada",
    )(c_rows, w_ada, b_ada)


def _norm_mod_kernel(*refs, parts, tr):
    n = len(parts)
    g_ref, sh_ref, sc_ref, o_ref = refs[n:]
    i = pl.program_id(0)
    for r0 in range(0, tr, NORM_ROWS):
        rows = slice(r0, r0 + NORM_ROWS)
        slab_rows = slice(r0 % ROW_GROUP, r0 % ROW_GROUP + NORM_ROWS)
        x = refs[0][rows, :]
        for ref, part in zip(refs[1:n], parts[1:]):
            x = jnp.where(i >= part[1] // tr, ref[rows, :], x)
        y = x * lax.rsqrt(jnp.mean(x * x, axis=-1, keepdims=True) + EPS) * g_ref[...]
        o_ref[rows, :] = (y * (1.0 + sc_ref[slab_rows, :]) + sh_ref[slab_rows, :]).astype(BF16)


def _norm_mod(parts, g, shift, scale):
    tr = 512 // len(parts)
    slab = lambda v: pl.BlockSpec((ROW_GROUP, D_MODEL), lambda i: (_slab_index(i, tr), v[1]))
    return pl.pallas_call(
        functools.partial(_norm_mod_kernel, parts=parts, tr=tr),
        out_shape=jax.ShapeDtypeStruct((M_ALL, D_MODEL), BF16),
        grid=(M_ALL // tr,),
        in_specs=_part_specs(parts, tr, D_MODEL, lambda: 0, 1) + [
            pl.BlockSpec((1, D_MODEL), lambda i: (0, 0)), slab(shift), slab(scale)],
        out_specs=pl.BlockSpec((tr, D_MODEL), lambda i: (i, 0)),
        compiler_params=_params(("parallel",)),
        name="norm_mod",
    )(*[p[0] for p in parts], g.reshape(1, D_MODEL), shift[0], scale[0])


def _final_norm_kernel(x_ref, g_ref, o_ref):
    x = x_ref[...]
    o_ref[...] = x * lax.rsqrt(jnp.mean(x * x, axis=-1, keepdims=True) + EPS) * g_ref[...]


def _final_norm(x, g, row0, rows, tr=512, seq_major_out=False):
    if seq_major_out:
        tr = DEC_BATCH
        out_shape = jax.ShapeDtypeStruct((DEC_BATCH, (rows // DEC_BATCH) * D_MODEL), F32)
        out_spec = pl.BlockSpec((tr, D_MODEL), lambda i: (0, i))
    else:
        out_shape = jax.ShapeDtypeStruct((rows, D_MODEL), F32)
        out_spec = pl.BlockSpec((tr, D_MODEL), lambda i: (i, 0))
    return pl.pallas_call(
        _final_norm_kernel,
        out_shape=out_shape,
        grid=(rows // tr,),
        in_specs=[pl.BlockSpec((tr, D_MODEL), lambda i: (i + row0 // tr, 0)),
                  pl.BlockSpec((1, D_MODEL), lambda i: (0, 0))],
        out_specs=out_spec,
        compiler_params=_params(("parallel",)),
        name="final_norm",
    )(x, g.reshape(1, D_MODEL))


def _resident_rows(tm, k, index_map):
    return pl.BlockSpec((tm, k), index_map, pipeline_mode=pl.Buffered(1))


def _grid_step():
    return pl.program_id(0) * pl.num_programs(1) + pl.program_id(1)


def _rounder_specs(rounders, n_inner, total_steps):
    in_specs, out_shapes, out_specs, static = [], [], [], []
    for src, col0, n_cols, cols, steps in rounders:
        assert total_steps >= steps and src.shape[0] % steps == 0 and n_cols % cols == 0
        rows = src.shape[0] // steps
        blk = lambda i, j, steps=steps: jnp.minimum(i * n_inner + j, steps - 1)
        in_specs.append(pl.BlockSpec((rows, src.shape[1]), lambda i, j, blk=blk: (blk(i, j), 0)))
        out_shapes.append(jax.ShapeDtypeStruct((n_cols // cols, src.shape[0], cols), BF16))
        out_specs.append(pl.BlockSpec((n_cols // cols, rows, cols),
                                      lambda i, j, blk=blk: (0, blk(i, j), 0)))
        static.append((col0, steps))
    return in_specs, out_shapes, out_specs, tuple(static)


def _run_rounders(static, src_refs, dst_refs):
    step = _grid_step()
    for (col0, steps), src_ref, dst_ref in zip(static, src_refs, dst_refs):
        @pl.when(step < steps)
        def _(col0=col0, src_ref=src_ref, dst_ref=dst_ref):
            cols = dst_ref.shape[2]
            for nb in range(dst_ref.shape[0]):
                lo = col0 + nb * cols
                dst_ref[nb] = src_ref[:, lo:lo + cols].astype(BF16)


def _mm_plain_kernel(*refs, rounders):
    n_r = len(rounders)
    x_ref, w_ref = refs[:2]
    o_ref = refs[2 + n_r]
    o_ref[...] = jnp.dot(x_ref[...], w_ref[...].astype(BF16), preferred_element_type=F32)
    _run_rounders(rounders, refs[2:2 + n_r], refs[3 + n_r:])


def _mm_plain(x, w, n, rounders=(), tm=1536, tn=512):
    m, k = x.shape
    nj = n // tn
    r_in, r_shapes, r_out, r_static = _rounder_specs(rounders, nj, (m // tm) * nj)
    return pl.pallas_call(
        functools.partial(_mm_plain_kernel, rounders=r_static),
        out_shape=(jax.ShapeDtypeStruct((m, n), F32), *r_shapes),
        grid=(m // tm, nj),
        in_specs=[pl.BlockSpec((tm, k), lambda i, j: (i, 0)),
                  pl.BlockSpec((k, tn), lambda i, j: (0, j))] + r_in,
        out_specs=(pl.BlockSpec((tm, tn), lambda i, j: (i, j)), *r_out),
        compiler_params=_params(("arbitrary", "arbitrary")),
        name="mm_qkv",
    )(x, w, *[r[0] for r in rounders])


def _mm_swiglu_kernel(*refs, rounders, ada_steps):
    n_r = len(rounders)
    n_ada = 3 if ada_steps else 0
    x_ref, w1_ref, w3_ref = refs[:3]
    n_in = 3 + n_ada + n_r
    o_ref = refs[n_in]
    x = x_ref[...]
    a = jnp.dot(x, w1_ref[...].astype(BF16), preferred_element_type=F32)
    b = jnp.dot(x, w3_ref[...].astype(BF16), preferred_element_type=F32)
    o_ref[...] = (a / (1.0 + jnp.exp(-a)) * b).astype(BF16)
    _run_rounders(rounders, refs[3 + n_ada:n_in], refs[len(refs) - n_r:])

    if ada_steps:
        @pl.when(_grid_step() < ada_steps)
        def _():
            refs[n_in + 1][...] = _ada_block(*refs[3:6])


W2_COLS = 512
ADA_COLS = 256


def _mm_swiglu(x, w1, w3, rounders=(), ada=None, tm=3072, tn=256):
    m, k = x.shape
    n = w1.shape[1]
    nj = n // tn
    steps = (m // tm) * nj
    in_specs = [_resident_rows(tm, k, lambda i, j: (i, 0)),
                pl.BlockSpec((k, tn), lambda i, j: (0, j)),
                pl.BlockSpec((k, tn), lambda i, j: (0, j))]
    out_shape = [jax.ShapeDtypeStruct((m, n), BF16)]
    out_specs = [pl.BlockSpec((tm, tn), lambda i, j: (i, j))]
    args = [x, w1, w3]
    ada_steps = 0
    if ada is not None:
        a, w_ada, b_ada, col0, n_ada = ada
        ada_steps = n_ada // ADA_COLS
        assert steps >= ada_steps
        ada_blk = lambda i, j: jnp.minimum(i * nj + j, ada_steps - 1)
        in_specs += [
            pl.BlockSpec(a.shape, lambda i, j: (0, 0), pipeline_mode=pl.Buffered(1)),
            pl.BlockSpec((k, ADA_COLS), lambda i, j: (0, col0 // ADA_COLS + ada_blk(i, j))),
            pl.BlockSpec((1, ADA_COLS), lambda i, j: (0, col0 // ADA_COLS + ada_blk(i, j)))]
        out_shape.append(jax.ShapeDtypeStruct((2 * ROW_GROUP, n_ada), F32))
        out_specs.append(pl.BlockSpec((2 * ROW_GROUP, ADA_COLS), lambda i, j: (0, ada_blk(i, j))))
        args += [a, w_ada, b_ada]
    r_in, r_shapes, r_out, r_static = _rounder_specs(rounders, nj, steps)
    return pl.pallas_call(
        functools.partial(_mm_swiglu_kernel, rounders=r_static, ada_steps=ada_steps),
        out_shape=(*out_shape, *r_shapes),
        grid=(m // tm, nj),
        in_specs=in_specs + r_in,
        out_specs=(*out_specs, *r_out),
        compiler_params=_params(("arbitrary", "arbitrary")),
        name="mm_swiglu",
    )(*args, *[r[0] for r in rounders])


def _mm_res_kernel(*refs, x_parts, res_parts, tm, coef, rounders):
    i = pl.program_id(0)
    pos = 0
    xs = []
    for parts in x_parts:
        xs.append(_pick_part(refs[pos:pos + len(parts)], parts, tm, i))
        pos += len(parts)
    w_ref = refs[pos]
    res = _pick_part(refs[pos + 1:pos + 1 + len(res_parts)], res_parts, tm, i)
    gate_ref = refs[pos + 1 + len(res_parts)]
    n_in = pos + 2 + len(res_parts) + len(rounders)
    o_ref = refs[n_in]
    x = xs[0] if len(xs) == 1 else jnp.concatenate(xs, axis=1)
    acc = jnp.dot(x, w_ref[...].astype(BF16), preferred_element_type=F32)
    tn = acc.shape[1]
    gate = coef * gate_ref[...]
    y = acc.reshape(tm // ROW_GROUP, ROW_GROUP, tn) * gate[None]
    o_ref[...] = res + y.reshape(tm, tn)
    _run_rounders(rounders, refs[n_in - len(rounders):n_in], refs[n_in + 1:])


def _mm_res(x_parts, w, res_parts, gate, coef, tm, tn, name, rounders=()):
    mod, k_gate = gate
    if w.ndim == 3:
        assert w.shape[2] == tn
        k, n = w.shape[1], w.shape[0] * tn
        w_spec = pl.BlockSpec((None, k, tn), lambda i, j: (j, 0, 0))
    else:
        k, n = w.shape
        w_spec = pl.BlockSpec((k, tn), lambda i, j: (0, j))
    gate_blocks = D_MODEL // tn
    nj = n // tn
    in_specs, args = [], []
    for parts in x_parts:
        in_specs += _part_specs(parts, tm, parts[0][0].shape[1], lambda j: 0, 2)
        args += [p[0] for p in parts]
    in_specs.append(w_spec)
    in_specs += _part_specs(res_parts, tm, tn, lambda j: j, 2)
    in_specs.append(pl.BlockSpec(
        (ROW_GROUP, tn), lambda i, j: (_slab_index(i, tm), k_gate * gate_blocks + j)))
    args += [w] + [p[0] for p in res_parts] + [mod]
    r_in, r_shapes, r_out, r_static = _rounder_specs(rounders, nj, (M_ALL // tm) * nj)
    return pl.pallas_call(
        functools.partial(_mm_res_kernel, x_parts=x_parts, res_parts=res_parts, tm=tm,
                          coef=coef, rounders=r_static),
        out_shape=(jax.ShapeDtypeStruct((M_ALL, n), F32), *r_shapes),
        grid=(M_ALL // tm, nj),
        in_specs=in_specs + r_in,
        out_specs=(pl.BlockSpec((tm, tn), lambda i, j: (i, j)), *r_out),
        compiler_params=_params(("arbitrary", "arbitrary")),
        name=name,
    )(*args, *[r[0] for r in rounders])


CONV_COLS = 256
TAIL_ROWS = (CONV_WIDTH - 1) * ROW_GROUP


def _mm_conv_kernel(x_ref, wb_ref, wc_ref, wh_ref, cw_ref, st_ref, yc_ref, tail_ref,
                    shift_scr, carry_scr):
    i = pl.program_id(0)
    c = pl.program_id(1)
    tm = x_ref.shape[0]
    x = x_ref[...]
    dot = lambda w_ref: jnp.dot(x, w_ref[...].astype(BF16), preferred_element_type=F32)
    gate_b = dot(wb_ref)
    u = dot(wc_ref) * dot(wh_ref)
    w0, w1, w2 = cw_ref[0:1, :], cw_ref[1:2, :], cw_ref[2:3, :]
    tail_ref[0] = u[tm - TAIL_ROWS:, :]
    n_prompt_blocks = M_PROMPT // tm

    @pl.when(i < n_prompt_blocks)
    def _():
        @pl.when(i == 0)
        def _():
            shift_scr[0:SUBLANES, :] = jnp.zeros((SUBLANES, CONV_COLS), F32)

        @pl.when(i > 0)
        def _():
            shift_scr[0:SUBLANES, :] = carry_scr[c]

        shift_scr[SUBLANES:SUBLANES + tm, :] = u
        carry_scr[c] = u[tm - SUBLANES:, :]
        u1 = shift_scr[SUBLANES - 1:SUBLANES - 1 + tm, :]
        u2 = shift_scr[SUBLANES - 2:SUBLANES - 2 + tm, :]
        yc_ref[...] = (gate_b * (u2 * w0 + u1 * w1 + u * w2)).astype(BF16)

    @pl.when(i >= n_prompt_blocks)
    def _():
        shape = (DEC_SEQ, DEC_BATCH, CONV_COLS)
        u3 = u.reshape(shape)
        st = st_ref[...]
        u1 = jnp.concatenate([st[1:2], u3[:DEC_SEQ - 1]], axis=0)
        u2 = jnp.concatenate([st[0:2], u3[:DEC_SEQ - 2]], axis=0)
        y = u2 * w0[None] + u1 * w1[None] + u3 * w2[None]
        yc_ref[...] = (gate_b.reshape(shape) * y).reshape(tm, CONV_COLS).astype(BF16)


def _mm_conv(x, w_in, conv_w, state, tm=M_SAMPLE):
    m, k = x.shape
    nc = CONV_DIM // CONV_COLS
    first = QKV_DIM // CONV_COLS
    wcol = lambda base: pl.BlockSpec((k, CONV_COLS), lambda i, c: (0, first + base + c))
    return pl.pallas_call(
        _mm_conv_kernel,
        out_shape=(jax.ShapeDtypeStruct((m, CONV_DIM), BF16),
                   jax.ShapeDtypeStruct((m // tm, TAIL_ROWS, CONV_DIM), F32)),
        grid=(m // tm, nc),
        in_specs=[pl.BlockSpec((tm, k), lambda i, c: (i, 0)),
                  wcol(0), wcol(nc), wcol(2 * nc),
                  pl.BlockSpec((CONV_WIDTH, CONV_COLS), lambda i, c: (0, c)),
                  pl.BlockSpec((CONV_WIDTH - 1, DEC_BATCH, CONV_COLS), lambda i, c: (0, 0, c))],
        out_specs=(pl.BlockSpec((tm, CONV_COLS), lambda i, c: (i, c)),
                   pl.BlockSpec((1, TAIL_ROWS, CONV_COLS), lambda i, c: (i, 0, c))),
        scratch_shapes=[pltpu.VMEM((tm + SUBLANES, CONV_COLS), F32),
                        pltpu.VMEM((nc, SUBLANES, CONV_COLS), F32)],
        compiler_params=_params(("arbitrary", "arbitrary")),
        name="mm_conv",
    )(x, w_in, w_in, w_in, conv_w, state)


def _build_bias(tab_ref, bucket, bias_scr):
    for h in range(N_HEADS):
        acc = jnp.zeros(bucket.shape, F32)
        for b in range(N_BUCKETS):
            acc = jnp.where(bucket == b, tab_ref[b * N_HEADS + h], acc)
        bias_scr[h] = acc


def _valid_mask(rows, has_prev):
    r = lax.broadcasted_iota(jnp.int32, (rows, 2 * BLOCK), 0)
    c = lax.broadcasted_iota(jnp.int32, (rows, 2 * BLOCK), 1)
    dist = BLOCK + r - c
    return (dist >= 0) & (dist <= WINDOW) & ((c >= BLOCK) | has_prev)


def _softmax_with_sink(s, sink):
    m = jnp.maximum(jnp.max(s, axis=-1, keepdims=True), sink)
    p = jnp.exp(s - m)
    denom = jnp.sum(p, axis=-1, keepdims=True) + jnp.exp(sink - m)
    return p / denom


QUERY_BLOCKS = 4


def _attn_prompt_kernel(tab_ref, sink_ref, bucket_ref, q_ref, kc_ref, kp_ref, vc_ref, vp_ref,
                        o_ref, bias_scr):
    i = pl.program_id(0)

    @pl.when(i == 0)
    def _():
        _build_bias(tab_ref, bucket_ref[...], bias_scr)

    for g in range(N_KV_HEADS):
        kv = slice(g * HEAD_DIM, (g + 1) * HEAD_DIM)
        heads = [g * GQA_GROUP + j for j in range(GQA_GROUP)]
        for b in range(QUERY_BLOCKS):
            rows = slice(b * BLOCK, (b + 1) * BLOCK)
            if b == 0:
                k_prev, v_prev = kp_ref[:, kv], vp_ref[:, kv]
                valid = _valid_mask(BLOCK, i > 0)
            else:
                before = slice((b - 1) * BLOCK, b * BLOCK)
                k_prev, v_prev = kc_ref[before, kv], vc_ref[before, kv]
                valid = _valid_mask(BLOCK, True)
            qs = jnp.concatenate(
                [q_ref[rows, h * HEAD_DIM:(h + 1) * HEAD_DIM] for h in heads], axis=0).astype(BF16)
            kk = jnp.concatenate([k_prev, kc_ref[rows, kv]], axis=0).astype(BF16)
            vv = jnp.concatenate([v_prev, vc_ref[rows, kv]], axis=0).astype(BF16)
            s = lax.dot_general(qs, kk, (((1,), (1,)), ((), ())), preferred_element_type=F32)
            ps = []
            for j, h in enumerate(heads):
                sj = s[j * BLOCK:(j + 1) * BLOCK] * SCALE + bias_scr[h]
                sj = jnp.where(valid, sj, NEG)
                ps.append(_softmax_with_sink(sj, sink_ref[h]))
            p = jnp.concatenate(ps, axis=0).astype(BF16)
            o = jnp.dot(p, vv, preferred_element_type=F32)
            for j, h in enumerate(heads):
                o_ref[rows, h * HEAD_DIM:(h + 1) * HEAD_DIM] = (
                    o[j * BLOCK:(j + 1) * BLOCK].astype(BF16))


def _attn_prompt(qkv, table, sinks, bucket):
    rows = QUERY_BLOCKS * BLOCK
    q_blk = 0
    k_blk = ATTN_DIM // KV_DIM
    v_blk = k_blk + 1
    prev = lambda i: jnp.maximum(QUERY_BLOCKS * i - 1, 0)
    smem = pl.BlockSpec(memory_space=pltpu.SMEM)
    return pl.pallas_call(
        _attn_prompt_kernel,
        out_shape=jax.ShapeDtypeStruct((M_PROMPT, ATTN_DIM), BF16),
        grid=(M_PROMPT // rows,),
        in_specs=[smem, smem,
                  pl.BlockSpec((BLOCK, 2 * BLOCK), lambda i: (0, 0)),
                  pl.BlockSpec((rows, ATTN_DIM), lambda i: (i, q_blk)),
                  pl.BlockSpec((rows, KV_DIM), lambda i: (i, k_blk)),
                  pl.BlockSpec((BLOCK, KV_DIM), lambda i: (prev(i), k_blk)),
                  pl.BlockSpec((rows, KV_DIM), lambda i: (i, v_blk)),
                  pl.BlockSpec((BLOCK, KV_DIM), lambda i: (prev(i), v_blk))],
        out_specs=pl.BlockSpec((rows, ATTN_DIM), lambda i: (i, 0)),
        scratch_shapes=[pltpu.VMEM((N_HEADS, BLOCK, 2 * BLOCK), F32)],
        compiler_params=_params(("arbitrary",)),
        name="attn_prompt",
    )(table, sinks, bucket, qkv, qkv, qkv, qkv, qkv)


def _attn_sample_kernel(tab_ref, sink_ref, bucket_ref, q_ref, kn_ref, vn_ref, ck_ref, cv_ref,
                        o_ref, kw_ref, vw_ref, bias_scr):
    i = pl.program_id(0)
    g_seq = q_ref.shape[0]
    wb = ck_ref.shape[1] // N_KV_HEADS

    @pl.when(i == 0)
    def _():
        _build_bias(tab_ref, bucket_ref[0:DEC_SEQ, :], bias_scr)

    valid = _valid_mask(DEC_SEQ, True)[None]
    pad = jnp.zeros((g_seq, 2 * BLOCK - wb - DEC_SEQ, HEAD_DIM), F32)
    for g in range(N_KV_HEADS):
        kv = slice(g * HEAD_DIM, (g + 1) * HEAD_DIM)
        head_rows = pl.ds(g, wb, stride=N_KV_HEADS)
        heads = [g * GQA_GROUP + j for j in range(GQA_GROUP)]
        qs = jnp.concatenate(
            [q_ref[:, :, h * HEAD_DIM:(h + 1) * HEAD_DIM] for h in heads], axis=1).astype(BF16)
        kk = jnp.concatenate([ck_ref[:, head_rows, :], kn_ref[:, :, kv], pad], axis=1).astype(BF16)
        vv = jnp.concatenate([cv_ref[:, head_rows, :], vn_ref[:, :, kv], pad], axis=1).astype(BF16)
        s = jnp.einsum('gqd,gkd->gqk', qs, kk, preferred_element_type=F32)
        ps = []
        for j, h in enumerate(heads):
            sj = s[:, j * DEC_SEQ:(j + 1) * DEC_SEQ] * SCALE + bias_scr[h][None]
            sj = jnp.where(valid, sj, NEG)
            ps.append(_softmax_with_sink(sj, sink_ref[h]))
        p = jnp.concatenate(ps, axis=1).astype(BF16)
        o = jnp.einsum('gqk,gkd->gqd', p, vv, preferred_element_type=F32)
        for j, h in enumerate(heads):
            o_ref[:, :, h * HEAD_DIM:(h + 1) * HEAD_DIM] = (
                o[:, j * DEC_SEQ:(j + 1) * DEC_SEQ].astype(BF16))
        new_rows = pl.ds((wb - DEC_SEQ) * N_KV_HEADS + g, DEC_SEQ, stride=N_KV_HEADS)
        kw_ref[:, new_rows, :] = kn_ref[:, :, kv]
        vw_ref[:, new_rows, :] = vn_ref[:, :, kv]
    kept = (wb - DEC_SEQ) * N_KV_HEADS
    kw_ref[:, 0:kept, :] = ck_ref[:, DEC_SEQ * N_KV_HEADS:, :]
    vw_ref[:, 0:kept, :] = cv_ref[:, DEC_SEQ * N_KV_HEADS:, :]


def _attn_sample(q, kn, vn, ck, cv, table, sinks, bucket, g_seq=16):
    rows = ck.shape[1]
    smem = pl.BlockSpec(memory_space=pltpu.SMEM)
    blk = lambda r, c: pl.BlockSpec((g_seq, r, c), lambda i: (i, 0, 0))
    return pl.pallas_call(
        _attn_sample_kernel,
        out_shape=(jax.ShapeDtypeStruct((DEC_BATCH, DEC_SEQ, ATTN_DIM), BF16),
                   jax.ShapeDtypeStruct((DEC_BATCH, rows, HEAD_DIM), F32),
                   jax.ShapeDtypeStruct((DEC_BATCH, rows, HEAD_DIM), F32)),
        grid=(DEC_BATCH // g_seq,),
        in_specs=[smem, smem,
                  pl.BlockSpec((BLOCK, 2 * BLOCK), lambda i: (0, 0)),
                  blk(DEC_SEQ, ATTN_DIM), blk(DEC_SEQ, KV_DIM), blk(DEC_SEQ, KV_DIM),
                  blk(rows, HEAD_DIM), blk(rows, HEAD_DIM)],
        out_specs=(blk(DEC_SEQ, ATTN_DIM), blk(rows, HEAD_DIM), blk(rows, HEAD_DIM)),
        scratch_shapes=[pltpu.VMEM((N_HEADS, DEC_SEQ, 2 * BLOCK), F32)],
        compiler_params=_params(("arbitrary",)),
        name="attn_sample",
    )(table, sinks, bucket, q, kn, vn, ck, cv)


def _to_step_major(a):
    return a.transpose(1, 0, 2).reshape(M_SAMPLE, a.shape[-1])


def _to_seq_major(a):
    return a.reshape(DEC_SEQ, DEC_BATCH, a.shape[-1]).transpose(1, 0, 2)


def _whole(a):
    return [(a, 0, M_ALL)]


N_MOD_EARLY = 2


def _ffn(x_parts, mods, k0, g, w1, w3, w2, ada_rest=None, rounders=()):
    h = _norm_mod(x_parts, g, mods[k0], mods[k0 + 1])
    sweep = w1.shape[1] // 256
    if ada_rest is None:
        u, w2_bf16 = _mm_swiglu(
            h, w1, w3, rounders=[(w2, 0, D_MODEL, W2_COLS, 2 * sweep)])
    else:
        n_rest = (N_MOD - N_MOD_EARLY) * D_MODEL
        u, mod_rest, w2_bf16 = _mm_swiglu(
            h, w1, w3, rounders=[(w2, 0, D_MODEL, W2_COLS, 4 * sweep)],
            ada=ada_rest + (N_MOD_EARLY * D_MODEL, n_rest), tm=2304)
        mods += [(mod_rest, k) for k in range(N_MOD - N_MOD_EARLY)]
    return _mm_res([_whole(u)], w2_bf16, x_parts, mods[k0 + 2], 0.5, tm=512, tn=W2_COLS,
                   name="mm_w2", rounders=rounders)


def kernel(x_prompt, x_sample, c_prompt, c_sample, cache_k, cache_v, state_conv, rel_bias,
           g_ffn1, w1_ffn1, w3_ffn1, w2_ffn1, g_mix, w_in, sinks, conv_w, w_out,
           g_ffn2, w1_ffn2, w3_ffn2, w2_ffn2, w_ada, b_ada, g_final):
    depth = w_in.shape[0]
    wb = cache_k.shape[2]
    bucket = jnp.asarray(_bucket_map())
    table = rel_bias.reshape(N_BUCKETS * N_HEADS)
    c_rows = jnp.concatenate(
        [c_sample, c_prompt, jnp.zeros((ADA_ROWS - ROW_GROUP - 1, D_MODEL), F32)], axis=0)

    x_parts = [(x_prompt.reshape(M_PROMPT, D_MODEL), 0, M_PROMPT),
               (_to_step_major(x_sample), M_PROMPT, M_SAMPLE)]

    kp, vp, cp, ks, vs, cs = [], [], [], [], [], []
    for l in range(depth):
        b_ada_l = b_ada[l].reshape(1, N_MOD * D_MODEL)
        mod_early, silu_c = _ada(c_rows, w_ada[l], b_ada_l, N_MOD_EARLY * D_MODEL)
        mods = [(mod_early, k) for k in range(N_MOD_EARLY)]
        x, = _ffn(x_parts, mods, 0, g_ffn1[l], w1_ffn1[l], w3_ffn1[l], w2_ffn1[l],
                  ada_rest=(silu_c, w_ada[l], b_ada_l))

        h = _norm_mod(_whole(x), g_mix[l], mods[3], mods[4])
        qkv, = _mm_plain(h, w_in[l], QKV_DIM)
        yc, tails = _mm_conv(h, w_in[l], conv_w[l], state_conv[l].transpose(1, 0, 2))

        o_p = _attn_prompt(qkv, table, sinks[l], bucket)
        qkv_s = qkv[M_PROMPT:]
        q_s = _to_seq_major(qkv_s[:, :ATTN_DIM])
        kn_s = _to_seq_major(qkv_s[:, ATTN_DIM:ATTN_DIM + KV_DIM])
        vn_s = _to_seq_major(qkv_s[:, ATTN_DIM + KV_DIM:])
        o_s, kw_s, vw_s = _attn_sample(
            q_s, kn_s, vn_s,
            cache_k[l].reshape(DEC_BATCH, wb * N_KV_HEADS, HEAD_DIM),
            cache_v[l].reshape(DEC_BATCH, wb * N_KV_HEADS, HEAD_DIM), table, sinks[l], bucket)
        o_parts = [(o_p, 0, M_PROMPT),
                   (o_s.reshape(DEC_BATCH, DEC_SEQ * ATTN_DIM), M_PROMPT, M_SAMPLE, True)]
        x, = _mm_res([o_parts, _whole(yc)], w_out[l], _whole(x), mods[5], 1.0, tm=1024,
                     tn=512, name="mm_out")

        x, = _ffn(_whole(x), mods, 6, g_ffn2[l], w1_ffn2[l], w3_ffn2[l], w2_ffn2[l])
        x_parts = _whole(x)

        k_p = qkv[M_PROMPT - wb:M_PROMPT, ATTN_DIM:ATTN_DIM + KV_DIM]
        v_p = qkv[M_PROMPT - wb:M_PROMPT, ATTN_DIM + KV_DIM:]
        kp.append(k_p.reshape(1, wb, N_KV_HEADS, HEAD_DIM))
        vp.append(v_p.reshape(1, wb, N_KV_HEADS, HEAD_DIM))
        last_prompt = M_PROMPT // M_SAMPLE - 1
        cp.append(tails[last_prompt, TAIL_ROWS - (CONV_WIDTH - 1):].reshape(
            1, CONV_WIDTH - 1, CONV_DIM))
        ks.append(kw_s.reshape(DEC_BATCH, wb, N_KV_HEADS, HEAD_DIM))
        vs.append(vw_s.reshape(DEC_BATCH, wb, N_KV_HEADS, HEAD_DIM))
        cs.append(tails[last_prompt + 1].reshape(
            CONV_WIDTH - 1, DEC_BATCH, CONV_DIM).transpose(1, 0, 2))

    y_prompt = _final_norm(x, g_final, 0, M_PROMPT).reshape(1, SEQ, D_MODEL)
    y_sample = _final_norm(x, g_final, M_PROMPT, M_SAMPLE, seq_major_out=True).reshape(
        DEC_BATCH, DEC_SEQ, D_MODEL)
    return (y_prompt, y_sample, jnp.stack(kp), jnp.stack(vp), jnp.stack(cp),
            jnp.stack(ks), jnp.stack(vs), jnp.stack(cs))
```

```python
import functools
import math

import numpy as np
import jax
import jax.numpy as jnp
from jax import lax
from jax.experimental import pallas as pl
from jax.experimental.pallas import tpu as pltpu

F32 = jnp.float32
BF16 = jnp.bfloat16

D_MODEL = 4096
SEQ = 8192
DEC_BATCH = 128
DEC_SEQ = 8
M_PROMPT = SEQ
M_SAMPLE = DEC_BATCH * DEC_SEQ
M_ALL = M_PROMPT + M_SAMPLE
HEAD_DIM = 128
ATTN_DIM = D_MODEL // 2
N_HEADS = ATTN_DIM // HEAD_DIM
N_KV_HEADS = 4
GQA_GROUP = N_HEADS // N_KV_HEADS
KV_DIM = N_KV_HEADS * HEAD_DIM
QKV_DIM = ATTN_DIM + 2 * KV_DIM
CONV_DIM = D_MODEL - ATTN_DIM
CONV_WIDTH = 3
WINDOW = 128
BLOCK = 128
N_BUCKETS = 32
MAX_DISTANCE = 128
N_MOD = 9
EPS = 1e-6
NEG = -1e30
SCALE = HEAD_DIM ** -0.5

ROW_GROUP = 128
SUBLANES = 8
NORM_ROWS = 8
VMEM_LIMIT_BYTES = 56 * 1024 * 1024


def _params(semantics):
    return pltpu.CompilerParams(dimension_semantics=semantics,
                                vmem_limit_bytes=VMEM_LIMIT_BYTES)


def _slab_index(row_block, rows_per_block):
    return (row_block < M_PROMPT // rows_per_block).astype(jnp.int32)


def _part_specs(parts, tr, cols, col_index, grid_rank):
    specs = []
    for part in parts:
        row0, rows = part[1], part[2]
        b0, nb = row0 // tr, rows // tr
        blk = lambda i, b0=b0, nb=nb: jnp.clip(i - b0, 0, nb - 1)
        if _is_seq_major(part):
            assert part[0].shape[1] == (rows // DEC_BATCH) * cols
            shape = (DEC_BATCH, (tr // DEC_BATCH) * cols)
            imap = ((lambda i, blk=blk: (0, blk(i))) if grid_rank == 1 else
                    (lambda i, j, blk=blk: (0, blk(i))))
        else:
            shape = (tr, cols)
            imap = ((lambda i, blk=blk: (blk(i), col_index())) if grid_rank == 1 else
                    (lambda i, j, blk=blk: (blk(i), col_index(j))))
        specs.append(pl.BlockSpec(shape, imap))
    return specs


def _is_seq_major(part):
    return len(part) > 3 and part[3]


def _read_part(ref, part, tr):
    v = ref[...]
    if _is_seq_major(part):
        steps = tr // DEC_BATCH
        cols = v.shape[1] // steps
        v = jnp.concatenate([v[:, t * cols:(t + 1) * cols] for t in range(steps)], axis=0)
    return v


def _pick_part(refs, parts, tr, row_block):
    v = _read_part(refs[0], parts[0], tr)
    for ref, part in zip(refs[1:], parts[1:]):
        v = jnp.where(row_block >= part[1] // tr, _read_part(ref, part, tr), v)
    return v


def _bucket_map():
    r = np.arange(BLOCK)[:, None]
    c = np.arange(2 * BLOCK)[None, :]
    n = np.maximum(BLOCK + r - c, 0)
    max_exact = N_BUCKETS // 2
    nf = np.maximum(n, 1).astype(np.float32)
    large = max_exact + (np.log(nf / max_exact) / math.log(MAX_DISTANCE / max_exact)
                         * (N_BUCKETS - max_exact)).astype(np.int32)
    large = np.minimum(large, N_BUCKETS - 1)
    return np.where(n < max_exact, n, large).astype(np.int32)


ADA_ROWS = ROW_GROUP + 16


def _ada_block(a_ref, w_ref, b_ref):
    r = jnp.dot(a_ref[...], w_ref[...].astype(BF16), preferred_element_type=F32) + b_ref[...]
    prompt = jnp.broadcast_to(r[ROW_GROUP:ROW_GROUP + 1], (ROW_GROUP, r.shape[1]))
    return jnp.concatenate([r[:ROW_GROUP], prompt], axis=0)


def _ada_kernel(c_ref, w_ref, b_ref, o_ref, a_ref):
    @pl.when(pl.program_id(0) == 0)
    def _():
        c = c_ref[...]
        a_ref[...] = (c / (1.0 + jnp.exp(-c))).astype(BF16)

    o_ref[...] = _ada_block(a_ref, w_ref, b_ref)


def _ada(c_rows, w_ada, b_ada, n, tn=512):
    return pl.pallas_call(
        _ada_kernel,
        out_shape=(jax.ShapeDtypeStruct((2 * ROW_GROUP, n), F32),
                   jax.ShapeDtypeStruct((ADA_ROWS, D_MODEL), BF16)),
        grid=(n // tn,),
        in_specs=[pl.BlockSpec((ADA_ROWS, D_MODEL), lambda j: (0, 0)),
                  pl.BlockSpec((D_MODEL, tn), lambda j: (0, j)),
                  pl.BlockSpec((1, tn), lambda j: (0, j))],
        out_specs=(pl.BlockSpec((2 * ROW_GROUP, tn), lambda j: (0, j)),
                   pl.BlockSpec((ADA_ROWS, D_MODEL), lambda j: (0, 0))),
        compiler_params=_params(("arbitrary",)),
        name="ada",
    )(c_rows, w_ada, b_ada)


def _norm_mod_kernel(*refs, parts, tr):
    n = len(parts)
    g_ref, sh_ref, sc_ref, o_ref = refs[n:]
    i = pl.program_id(0)
    for r0 in range(0, tr, NORM_ROWS):
        rows = slice(r0, r0 + NORM_ROWS)
        slab_rows = slice(r0 % ROW_GROUP, r0 % ROW_GROUP + NORM_ROWS)
        x = refs[0][rows, :]
        for ref, part in zip(refs[1:n], parts[1:]):
            x = jnp.where(i >= part[1] // tr, ref[rows, :], x)
        y = x * lax.rsqrt(jnp.mean(x * x, axis=-1, keepdims=True) + EPS) * g_ref[...]
        o_ref[rows, :] = (y * (1.0 + sc_ref[slab_rows, :]) + sh_ref[slab_rows, :]).astype(BF16)


def _norm_mod(parts, g, shift, scale):
    tr = 512 // len(parts)
    slab = lambda v: pl.BlockSpec((ROW_GROUP, D_MODEL), lambda i: (_slab_index(i, tr), v[1]))
    return pl.pallas_call(
        functools.partial(_norm_mod_kernel, parts=parts, tr=tr),
        out_shape=jax.ShapeDtypeStruct((M_ALL, D_MODEL), BF16),
        grid=(M_ALL // tr,),
        in_specs=_part_specs(parts, tr, D_MODEL, lambda: 0, 1) + [
            pl.BlockSpec((1, D_MODEL), lambda i: (0, 0)), slab(shift), slab(scale)],
        out_specs=pl.BlockSpec((tr, D_MODEL), lambda i: (i, 0)),
        compiler_params=_params(("parallel",)),
        name="norm_mod",
    )(*[p[0] for p in parts], g.reshape(1, D_MODEL), shift[0], scale[0])


def _final_norm_kernel(x_ref, g_ref, o_ref):
    x = x_ref[...]
    o_ref[...] = x * lax.rsqrt(jnp.mean(x * x, axis=-1, keepdims=True) + EPS) * g_ref[...]


def _final_norm(x, g, row0, rows, tr=512, seq_major_out=False):
    if seq_major_out:
        tr = DEC_BATCH
        out_shape = jax.ShapeDtypeStruct((DEC_BATCH, (rows // DEC_BATCH) * D_MODEL), F32)
        out_spec = pl.BlockSpec((tr, D_MODEL), lambda i: (0, i))
    else:
        out_shape = jax.ShapeDtypeStruct((rows, D_MODEL), F32)
        out_spec = pl.BlockSpec((tr, D_MODEL), lambda i: (i, 0))
    return pl.pallas_call(
        _final_norm_kernel,
        out_shape=out_shape,
        grid=(rows // tr,),
        in_specs=[pl.BlockSpec((tr, D_MODEL), lambda i: (i + row0 // tr, 0)),
                  pl.BlockSpec((1, D_MODEL), lambda i: (0, 0))],
        out_specs=out_spec,
        compiler_params=_params(("parallel",)),
        name="final_norm",
    )(x, g.reshape(1, D_MODEL))


def _resident_rows(tm, k, index_map):
    return pl.BlockSpec((tm, k), index_map, pipeline_mode=pl.Buffered(1))


def _grid_step():
    return pl.program_id(0) * pl.num_programs(1) + pl.program_id(1)


def _rounder_specs(rounders, n_inner, total_steps):
    in_specs, out_shapes, out_specs, static = [], [], [], []
    for src, col0, n_cols, cols, steps in rounders:
        assert total_steps >= steps and src.shape[0] % steps == 0 and n_cols % cols == 0
        rows = src.shape[0] // steps
        blk = lambda i, j, steps=steps: jnp.minimum(i * n_inner + j, steps - 1)
        in_specs.append(pl.BlockSpec((rows, src.shape[1]), lambda i, j, blk=blk: (blk(i, j), 0)))
        out_shapes.append(jax.ShapeDtypeStruct((n_cols // cols, src.shape[0], cols), BF16))
        out_specs.append(pl.BlockSpec((n_cols // cols, rows, cols),
                                      lambda i, j, blk=blk: (0, blk(i, j), 0)))
        static.append((col0, steps))
    return in_specs, out_shapes, out_specs, tuple(static)


def _run_rounders(static, src_refs, dst_refs):
    step = _grid_step()
    for (col0, steps), src_ref, dst_ref in zip(static, src_refs, dst_refs):
        @pl.when(step < steps)
        def _(col0=col0, src_ref=src_ref, dst_ref=dst_ref):
            cols = dst_ref.shape[2]
            for nb in range(dst_ref.shape[0]):
                lo = col0 + nb * cols
                dst_ref[nb] = src_ref[:, lo:lo + cols].astype(BF16)


def _mm_plain_kernel(x_ref, w_ref, o_ref):
    o_ref[...] = jnp.dot(x_ref[...], w_ref[...].astype(BF16), preferred_element_type=F32)


def _mm_plain(x, w, n, tm=1536, tn=512):
    m, k = x.shape
    return pl.pallas_call(
        _mm_plain_kernel,
        out_shape=jax.ShapeDtypeStruct((m, n), F32),
        grid=(m // tm, n // tn),
        in_specs=[pl.BlockSpec((tm, k), lambda i, j: (i, 0)),
                  pl.BlockSpec((k, tn), lambda i, j: (0, j))],
        out_specs=pl.BlockSpec((tm, tn), lambda i, j: (i, j)),
        compiler_params=_params(("parallel", "parallel")),
        name="mm_qkv",
    )(x, w)


def _mm_swiglu_kernel(*refs, rounders, ada_steps):
    n_r = len(rounders)
    n_ada = 3 if ada_steps else 0
    x_ref, w1_ref, w3_ref = refs[:3]
    n_in = 3 + n_ada + n_r
    o_ref = refs[n_in]
    x = x_ref[...]
    a = jnp.dot(x, w1_ref[...].astype(BF16), preferred_element_type=F32)
    b = jnp.dot(x, w3_ref[...].astype(BF16), preferred_element_type=F32)
    o_ref[...] = (a / (1.0 + jnp.exp(-a)) * b).astype(BF16)
    _run_rounders(rounders, refs[3 + n_ada:n_in], refs[len(refs) - n_r:])

    if ada_steps:
        @pl.when(_grid_step() < ada_steps)
        def _():
            refs[n_in + 1][...] = _ada_block(*refs[3:6])


W2_COLS = 512
ADA_COLS = 256


def _mm_swiglu(x, w1, w3, rounders=(), ada=None, tm=3072, tn=256):
    m, k = x.shape
    n = w1.shape[1]
    nj = n // tn
    steps = (m // tm) * nj
    in_specs = [_resident_rows(tm, k, lambda i, j: (i, 0)),
                pl.BlockSpec((k, tn), lambda i, j: (0, j)),
                pl.BlockSpec((k, tn), lambda i, j: (0, j))]
    out_shape = [jax.ShapeDtypeStruct((m, n), BF16)]
    out_specs = [pl.BlockSpec((tm, tn), lambda i, j: (i, j))]
    args = [x, w1, w3]
    ada_steps = 0
    if ada is not None:
        a, w_ada, b_ada, col0, n_ada = ada
        ada_steps = n_ada // ADA_COLS
        assert steps >= ada_steps
        ada_blk = lambda i, j: jnp.minimum(i * nj + j, ada_steps - 1)
        in_specs += [
            pl.BlockSpec(a.shape, lambda i, j: (0, 0), pipeline_mode=pl.Buffered(1)),
            pl.BlockSpec((k, ADA_COLS), lambda i, j: (0, col0 // ADA_COLS + ada_blk(i, j))),
            pl.BlockSpec((1, ADA_COLS), lambda i, j: (0, col0 // ADA_COLS + ada_blk(i, j)))]
        out_shape.append(jax.ShapeDtypeStruct((2 * ROW_GROUP, n_ada), F32))
        out_specs.append(pl.BlockSpec((2 * ROW_GROUP, ADA_COLS), lambda i, j: (0, ada_blk(i, j))))
        args += [a, w_ada, b_ada]
    r_in, r_shapes, r_out, r_static = _rounder_specs(rounders, nj, steps)
    return pl.pallas_call(
        functools.partial(_mm_swiglu_kernel, rounders=r_static, ada_steps=ada_steps),
        out_shape=(*out_shape, *r_shapes),
        grid=(m // tm, nj),
        in_specs=in_specs + r_in,
        out_specs=(*out_specs, *r_out),
        compiler_params=_params(("arbitrary", "arbitrary")),
        name="mm_swiglu",
    )(*args, *[r[0] for r in rounders])


def _mm_res_kernel(*refs, x_parts, res_parts, tm, coef):
    i = pl.program_id(0)
    pos = 0
    xs = []
    for parts in x_parts:
        xs.append(_pick_part(refs[pos:pos + len(parts)], parts, tm, i))
        pos += len(parts)
    w_ref = refs[pos]
    res = _pick_part(refs[pos + 1:pos + 1 + len(res_parts)], res_parts, tm, i)
    gate_ref, o_ref = refs[-2:]
    x = xs[0] if len(xs) == 1 else jnp.concatenate(xs, axis=1)
    acc = jnp.dot(x, w_ref[...].astype(BF16), preferred_element_type=F32)
    tn = acc.shape[1]
    gate = coef * gate_ref[...]
    y = acc.reshape(tm // ROW_GROUP, ROW_GROUP, tn) * gate[None]
    o_ref[...] = res + y.reshape(tm, tn)


def _mm_res(x_parts, w, res_parts, gate, coef, tm, tn, name):
    mod, k_gate = gate
    if w.ndim == 3:
        assert w.shape[2] == tn
        k, n = w.shape[1], w.shape[0] * tn
        w_spec = pl.BlockSpec((None, k, tn), lambda i, j: (j, 0, 0))
    else:
        k, n = w.shape
        w_spec = pl.BlockSpec((k, tn), lambda i, j: (0, j))
    gate_blocks = D_MODEL // tn
    in_specs, args = [], []
    for parts in x_parts:
        in_specs += _part_specs(parts, tm, parts[0][0].shape[1], lambda j: 0, 2)
        args += [p[0] for p in parts]
    in_specs.append(w_spec)
    in_specs += _part_specs(res_parts, tm, tn, lambda j: j, 2)
    in_specs.append(pl.BlockSpec(
        (ROW_GROUP, tn), lambda i, j: (_slab_index(i, tm), k_gate * gate_blocks + j)))
    args += [w] + [p[0] for p in res_parts] + [mod]
    return pl.pallas_call(
        functools.partial(_mm_res_kernel, x_parts=x_parts, res_parts=res_parts, tm=tm,
                          coef=coef),
        out_shape=jax.ShapeDtypeStruct((M_ALL, n), F32),
        grid=(M_ALL // tm, n // tn),
        in_specs=in_specs,
        out_specs=pl.BlockSpec((tm, tn), lambda i, j: (i, j)),
        compiler_params=_params(("parallel", "parallel")),
        name=name,
    )(*args)


CONV_COLS = 256
TAIL_ROWS = (CONV_WIDTH - 1) * ROW_GROUP


def _mm_conv_kernel(x_ref, wb_ref, wc_ref, wh_ref, cw_ref, st_ref, yc_ref, tail_ref,
                    shift_scr, carry_scr):
    i = pl.program_id(0)
    c = pl.program_id(1)
    tm = x_ref.shape[0]
    x = x_ref[...]
    dot = lambda w_ref: jnp.dot(x, w_ref[...].astype(BF16), preferred_element_type=F32)
    gate_b = dot(wb_ref)
    u = dot(wc_ref) * dot(wh_ref)
    w0, w1, w2 = cw_ref[0:1, :], cw_ref[1:2, :], cw_ref[2:3, :]
    tail_ref[0] = u[tm - TAIL_ROWS:, :]
    n_prompt_blocks = M_PROMPT // tm

    @pl.when(i < n_prompt_blocks)
    def _():
        @pl.when(i == 0)
        def _():
            shift_scr[0:SUBLANES, :] = jnp.zeros((SUBLANES, CONV_COLS), F32)

        @pl.when(i > 0)
        def _():
            shift_scr[0:SUBLANES, :] = carry_scr[c]

        shift_scr[SUBLANES:SUBLANES + tm, :] = u
        carry_scr[c] = u[tm - SUBLANES:, :]
        u1 = shift_scr[SUBLANES - 1:SUBLANES - 1 + tm, :]
        u2 = shift_scr[SUBLANES - 2:SUBLANES - 2 + tm, :]
        yc_ref[...] = (gate_b * (u2 * w0 + u1 * w1 + u * w2)).astype(BF16)

    @pl.when(i >= n_prompt_blocks)
    def _():
        shape = (DEC_SEQ, DEC_BATCH, CONV_COLS)
        u3 = u.reshape(shape)
        st = st_ref[...]
        u1 = jnp.concatenate([st[1:2], u3[:DEC_SEQ - 1]], axis=0)
        u2 = jnp.concatenate([st[0:2], u3[:DEC_SEQ - 2]], axis=0)
        y = u2 * w0[None] + u1 * w1[None] + u3 * w2[None]
        yc_ref[...] = (gate_b.reshape(shape) * y).reshape(tm, CONV_COLS).astype(BF16)


def _mm_conv(x, w_in, conv_w, state, tm=M_SAMPLE):
    m, k = x.shape
    nc = CONV_DIM // CONV_COLS
    first = QKV_DIM // CONV_COLS
    wcol = lambda base: pl.BlockSpec((k, CONV_COLS), lambda i, c: (0, first + base + c))
    return pl.pallas_call(
        _mm_conv_kernel,
        out_shape=(jax.ShapeDtypeStruct((m, CONV_DIM), BF16),
                   jax.ShapeDtypeStruct((m // tm, TAIL_ROWS, CONV_DIM), F32)),
        grid=(m // tm, nc),
        in_specs=[pl.BlockSpec((tm, k), lambda i, c: (i, 0)),
                  wcol(0), wcol(nc), wcol(2 * nc),
                  pl.BlockSpec((CONV_WIDTH, CONV_COLS), lambda i, c: (0, c)),
                  pl.BlockSpec((CONV_WIDTH - 1, DEC_BATCH, CONV_COLS), lambda i, c: (0, 0, c))],
        out_specs=(pl.BlockSpec((tm, CONV_COLS), lambda i, c: (i, c)),
                   pl.BlockSpec((1, TAIL_ROWS, CONV_COLS), lambda i, c: (i, 0, c))),
        scratch_shapes=[pltpu.VMEM((tm + SUBLANES, CONV_COLS), F32),
                        pltpu.VMEM((nc, SUBLANES, CONV_COLS), F32)],
        compiler_params=_params(("arbitrary", "arbitrary")),
        name="mm_conv",
    )(x, w_in, w_in, w_in, conv_w, state)


def _build_bias(tab_ref, bucket, bias_scr):
    for h in range(N_HEADS):
        acc = jnp.zeros(bucket.shape, F32)
        for b in range(N_BUCKETS):
            acc = jnp.where(bucket == b, tab_ref[b * N_HEADS + h], acc)
        bias_scr[h] = acc


def _valid_mask(rows, has_prev):
    r = lax.broadcasted_iota(jnp.int32, (rows, 2 * BLOCK), 0)
    c = lax.broadcasted_iota(jnp.int32, (rows, 2 * BLOCK), 1)
    dist = BLOCK + r - c
    return (dist >= 0) & (dist <= WINDOW) & ((c >= BLOCK) | has_prev)


def _softmax_with_sink(s, sink):
    m = jnp.maximum(jnp.max(s, axis=-1, keepdims=True), sink)
    p = jnp.exp(s - m)
    denom = jnp.sum(p, axis=-1, keepdims=True) + jnp.exp(sink - m)
    return p / denom


QUERY_BLOCKS = 4


def _attn_prompt_kernel(tab_ref, sink_ref, bucket_ref, q_ref, kc_ref, kp_ref, vc_ref, vp_ref,
                        o_ref, bias_scr):
    i = pl.program_id(0)

    @pl.when(i == 0)
    def _():
        _build_bias(tab_ref, bucket_ref[...], bias_scr)

    for g in range(N_KV_HEADS):
        kv = slice(g * HEAD_DIM, (g + 1) * HEAD_DIM)
        heads = [g * GQA_GROUP + j for j in range(GQA_GROUP)]
        for b in range(QUERY_BLOCKS):
            rows = slice(b * BLOCK, (b + 1) * BLOCK)
            if b == 0:
                k_prev, v_prev = kp_ref[:, kv], vp_ref[:, kv]
                valid = _valid_mask(BLOCK, i > 0)
            else:
                before = slice((b - 1) * BLOCK, b * BLOCK)
                k_prev, v_prev = kc_ref[before, kv], vc_ref[before, kv]
                valid = _valid_mask(BLOCK, True)
            qs = jnp.concatenate(
                [q_ref[rows, h * HEAD_DIM:(h + 1) * HEAD_DIM] for h in heads], axis=0).astype(BF16)
            kk = jnp.concatenate([k_prev, kc_ref[rows, kv]], axis=0).astype(BF16)
            vv = jnp.concatenate([v_prev, vc_ref[rows, kv]], axis=0).astype(BF16)
            s = lax.dot_general(qs, kk, (((1,), (1,)), ((), ())), preferred_element_type=F32)
            ps = []
            for j, h in enumerate(heads):
                sj = s[j * BLOCK:(j + 1) * BLOCK] * SCALE + bias_scr[h]
                sj = jnp.where(valid, sj, NEG)
                ps.append(_softmax_with_sink(sj, sink_ref[h]))
            p = jnp.concatenate(ps, axis=0).astype(BF16)
            o = jnp.dot(p, vv, preferred_element_type=F32)
            for j, h in enumerate(heads):
                o_ref[rows, h * HEAD_DIM:(h + 1) * HEAD_DIM] = (
                    o[j * BLOCK:(j + 1) * BLOCK].astype(BF16))


def _attn_prompt(qkv, table, sinks, bucket):
    rows = QUERY_BLOCKS * BLOCK
    q_blk = 0
    k_blk = ATTN_DIM // KV_DIM
    v_blk = k_blk + 1
    prev = lambda i: jnp.maximum(QUERY_BLOCKS * i - 1, 0)
    smem = pl.BlockSpec(memory_space=pltpu.SMEM)
    return pl.pallas_call(
        _attn_prompt_kernel,
        out_shape=jax.ShapeDtypeStruct((M_PROMPT, ATTN_DIM), BF16),
        grid=(M_PROMPT // rows,),
        in_specs=[smem, smem,
                  pl.BlockSpec((BLOCK, 2 * BLOCK), lambda i: (0, 0)),
                  pl.BlockSpec((rows, ATTN_DIM), lambda i: (i, q_blk)),
                  pl.BlockSpec((rows, KV_DIM), lambda i: (i, k_blk)),
                  pl.BlockSpec((BLOCK, KV_DIM), lambda i: (prev(i), k_blk)),
                  pl.BlockSpec((rows, KV_DIM), lambda i: (i, v_blk)),
                  pl.BlockSpec((BLOCK, KV_DIM), lambda i: (prev(i), v_blk))],
        out_specs=pl.BlockSpec((rows, ATTN_DIM), lambda i: (i, 0)),
        scratch_shapes=[pltpu.VMEM((N_HEADS, BLOCK, 2 * BLOCK), F32)],
        compiler_params=_params(("arbitrary",)),
        name="attn_prompt",
    )(table, sinks, bucket, qkv, qkv, qkv, qkv, qkv)


def _attn_sample_kernel(tab_ref, sink_ref, bucket_ref, q_ref, kn_ref, vn_ref, ck_ref, cv_ref,
                        o_ref, kw_ref, vw_ref, bias_scr):
    i = pl.program_id(0)
    g_seq = q_ref.shape[0]
    wb = ck_ref.shape[1] // N_KV_HEADS

    @pl.when(i == 0)
    def _():
        _build_bias(tab_ref, bucket_ref[0:DEC_SEQ, :], bias_scr)

    valid = _valid_mask(DEC_SEQ, True)[None]
    pad = jnp.zeros((g_seq, 2 * BLOCK - wb - DEC_SEQ, HEAD_DIM), F32)
    for g in range(N_KV_HEADS):
        kv = slice(g * HEAD_DIM, (g + 1) * HEAD_DIM)
        head_rows = pl.ds(g, wb, stride=N_KV_HEADS)
        heads = [g * GQA_GROUP + j for j in range(GQA_GROUP)]
        qs = jnp.concatenate(
            [q_ref[:, :, h * HEAD_DIM:(h + 1) * HEAD_DIM] for h in heads], axis=1).astype(BF16)
        kk = jnp.concatenate([ck_ref[:, head_rows, :], kn_ref[:, :, kv], pad], axis=1).astype(BF16)
        vv = jnp.concatenate([cv_ref[:, head_rows, :], vn_ref[:, :, kv], pad], axis=1).astype(BF16)
        s = jnp.einsum('gqd,gkd->gqk', qs, kk, preferred_element_type=F32)
        ps = []
        for j, h in enumerate(heads):
            sj = s[:, j * DEC_SEQ:(j + 1) * DEC_SEQ] * SCALE + bias_scr[h][None]
            sj = jnp.where(valid, sj, NEG)
            ps.append(_softmax_with_sink(sj, sink_ref[h]))
        p = jnp.concatenate(ps, axis=1).astype(BF16)
        o = jnp.einsum('gqk,gkd->gqd', p, vv, preferred_element_type=F32)
        for j, h in enumerate(heads):
            o_ref[:, :, h * HEAD_DIM:(h + 1) * HEAD_DIM] = (
                o[:, j * DEC_SEQ:(j + 1) * DEC_SEQ].astype(BF16))
        new_rows = pl.ds((wb - DEC_SEQ) * N_KV_HEADS + g, DEC_SEQ, stride=N_KV_HEADS)
        kw_ref[:, new_rows, :] = kn_ref[:, :, kv]
        vw_ref[:, new_rows, :] = vn_ref[:, :, kv]
    kept = (wb - DEC_SEQ) * N_KV_HEADS
    kw_ref[:, 0:kept, :] = ck_ref[:, DEC_SEQ * N_KV_HEADS:, :]
    vw_ref[:, 0:kept, :] = cv_ref[:, DEC_SEQ * N_KV_HEADS:, :]


def _attn_sample(q, kn, vn, ck, cv, table, sinks, bucket, g_seq=16):
    rows = ck.shape[1]
    smem = pl.BlockSpec(memory_space=pltpu.SMEM)
    blk = lambda r, c: pl.BlockSpec((g_seq, r, c), lambda i: (i, 0, 0))
    return pl.pallas_call(
        _attn_sample_kernel,
        out_shape=(jax.ShapeDtypeStruct((DEC_BATCH, DEC_SEQ, ATTN_DIM), BF16),
                   jax.ShapeDtypeStruct((DEC_BATCH, rows, HEAD_DIM), F32),
                   jax.ShapeDtypeStruct((DEC_BATCH, rows, HEAD_DIM), F32)),
        grid=(DEC_BATCH // g_seq,),
        in_specs=[smem, smem,
                  pl.BlockSpec((BLOCK, 2 * BLOCK), lambda i: (0, 0)),
                  blk(DEC_SEQ, ATTN_DIM), blk(DEC_SEQ, KV_DIM), blk(DEC_SEQ, KV_DIM),
                  blk(rows, HEAD_DIM), blk(rows, HEAD_DIM)],
        out_specs=(blk(DEC_SEQ, ATTN_DIM), blk(rows, HEAD_DIM), blk(rows, HEAD_DIM)),
        scratch_shapes=[pltpu.VMEM((N_HEADS, DEC_SEQ, 2 * BLOCK), F32)],
        compiler_params=_params(("arbitrary",)),
        name="attn_sample",
    )(table, sinks, bucket, q, kn, vn, ck, cv)


def _to_step_major(a):
    return a.transpose(1, 0, 2).reshape(M_SAMPLE, a.shape[-1])


def _to_seq_major(a):
    return a.reshape(DEC_SEQ, DEC_BATCH, a.shape[-1]).transpose(1, 0, 2)


def _whole(a):
    return [(a, 0, M_ALL)]


N_MOD_EARLY = 2


def _ffn(x_parts, mods, k0, g, w1, w3, w2, ada_rest=None):
    h = _norm_mod(x_parts, g, mods[k0], mods[k0 + 1])
    sweep = w1.shape[1] // 256
    if ada_rest is None:
        u, w2_bf16 = _mm_swiglu(
            h, w1, w3, rounders=[(w2, 0, D_MODEL, W2_COLS, 2 * sweep)])
    else:
        n_rest = (N_MOD - N_MOD_EARLY) * D_MODEL
        u, mod_rest, w2_bf16 = _mm_swiglu(
            h, w1, w3, rounders=[(w2, 0, D_MODEL, W2_COLS, 4 * sweep)],
            ada=ada_rest + (N_MOD_EARLY * D_MODEL, n_rest), tm=2304)
        mods += [(mod_rest, k) for k in range(N_MOD - N_MOD_EARLY)]
    return _mm_res([_whole(u)], w2_bf16, x_parts, mods[k0 + 2], 0.5, tm=512, tn=W2_COLS,
                   name="mm_w2")


def kernel(x_prompt, x_sample, c_prompt, c_sample, cache_k, cache_v, state_conv, rel_bias,
           g_ffn1, w1_ffn1, w3_ffn1, w2_ffn1, g_mix, w_in, sinks, conv_w, w_out,
           g_ffn2, w1_ffn2, w3_ffn2, w2_ffn2, w_ada, b_ada, g_final):
    depth = w_in.shape[0]
    wb = cache_k.shape[2]
    bucket = jnp.asarray(_bucket_map())
    table = rel_bias.reshape(N_BUCKETS * N_HEADS)
    c_rows = jnp.concatenate(
        [c_sample, c_prompt, jnp.zeros((ADA_ROWS - ROW_GROUP - 1, D_MODEL), F32)], axis=0)

    x_parts = [(x_prompt.reshape(M_PROMPT, D_MODEL), 0, M_PROMPT),
               (_to_step_major(x_sample), M_PROMPT, M_SAMPLE)]

    kp, vp, cp, ks, vs, cs = [], [], [], [], [], []
    for l in range(depth):
        b_ada_l = b_ada[l].reshape(1, N_MOD * D_MODEL)
        mod_early, silu_c = _ada(c_rows, w_ada[l], b_ada_l, N_MOD_EARLY * D_MODEL)
        mods = [(mod_early, k) for k in range(N_MOD_EARLY)]
        x = _ffn(x_parts, mods, 0, g_ffn1[l], w1_ffn1[l], w3_ffn1[l], w2_ffn1[l],
                 ada_rest=(silu_c, w_ada[l], b_ada_l))

        h = _norm_mod(_whole(x), g_mix[l], mods[3], mods[4])
        qkv = _mm_plain(h, w_in[l], QKV_DIM)
        yc, tails = _mm_conv(h, w_in[l], conv_w[l], state_conv[l].transpose(1, 0, 2))

        o_p = _attn_prompt(qkv, table, sinks[l], bucket)
        qkv_s = qkv[M_PROMPT:]
        q_s = _to_seq_major(qkv_s[:, :ATTN_DIM])
        kn_s = _to_seq_major(qkv_s[:, ATTN_DIM:ATTN_DIM + KV_DIM])
        vn_s = _to_seq_major(qkv_s[:, ATTN_DIM + KV_DIM:])
        o_s, kw_s, vw_s = _attn_sample(
            q_s, kn_s, vn_s,
            cache_k[l].reshape(DEC_BATCH, wb * N_KV_HEADS, HEAD_DIM),
            cache_v[l].reshape(DEC_BATCH, wb * N_KV_HEADS, HEAD_DIM), table, sinks[l], bucket)
        o_parts = [(o_p, 0, M_PROMPT),
                   (o_s.reshape(DEC_BATCH, DEC_SEQ * ATTN_DIM), M_PROMPT, M_SAMPLE, True)]
        x = _mm_res([o_parts, _whole(yc)], w_out[l], _whole(x), mods[5], 1.0, tm=1024,
                    tn=512, name="mm_out")

        x = _ffn(_whole(x), mods, 6, g_ffn2[l], w1_ffn2[l], w3_ffn2[l], w2_ffn2[l])
        x_parts = _whole(x)

        k_p = qkv[M_PROMPT - wb:M_PROMPT, ATTN_DIM:ATTN_DIM + KV_DIM]
        v_p = qkv[M_PROMPT - wb:M_PROMPT, ATTN_DIM + KV_DIM:]
        kp.append(k_p.reshape(1, wb, N_KV_HEADS, HEAD_DIM))
        vp.append(v_p.reshape(1, wb, N_KV_HEADS, HEAD_DIM))
        last_prompt = M_PROMPT // M_SAMPLE - 1
        cp.append(tails[last_prompt, TAIL_ROWS - (CONV_WIDTH - 1):].reshape(
            1, CONV_WIDTH - 1, CONV_DIM))
        ks.append(kw_s.reshape(DEC_BATCH, wb, N_KV_HEADS, HEAD_DIM))
        vs.append(vw_s.reshape(DEC_BATCH, wb, N_KV_HEADS, HEAD_DIM))
        cs.append(tails[last_prompt + 1].reshape(
            CONV_WIDTH - 1, DEC_BATCH, CONV_DIM).transpose(1, 0, 2))

    y_prompt = _final_norm(x, g_final, 0, M_PROMPT).reshape(1, SEQ, D_MODEL)
    y_sample = _final_norm(x, g_final, M_PROMPT, M_SAMPLE, seq_major_out=True).reshape(
        DEC_BATCH, DEC_SEQ, D_MODEL)
    return (y_prompt, y_sample, jnp.stack(kp), jnp.stack(vp), jnp.stack(cp),
            jnp.stack(ks), jnp.stack(vs), jnp.stack(cs))
```

```python
import functools
import math

import numpy as np
import jax
import jax.numpy as jnp
from jax import lax
from jax.experimental import pallas as pl
from jax.experimental.pallas import tpu as pltpu

F32 = jnp.float32
BF16 = jnp.bfloat16

D_MODEL = 4096
SEQ = 8192
DEC_BATCH = 128
DEC_SEQ = 8
M_PROMPT = SEQ
M_SAMPLE = DEC_BATCH * DEC_SEQ
M_ALL = M_PROMPT + M_SAMPLE
HEAD_DIM = 128
ATTN_DIM = D_MODEL // 2
N_HEADS = ATTN_DIM // HEAD_DIM
N_KV_HEADS = 4
GQA_GROUP = N_HEADS // N_KV_HEADS
KV_DIM = N_KV_HEADS * HEAD_DIM
QKV_DIM = ATTN_DIM + 2 * KV_DIM
CONV_DIM = D_MODEL - ATTN_DIM
CONV_WIDTH = 3
WINDOW = 128
BLOCK = 128
N_BUCKETS = 32
MAX_DISTANCE = 128
N_MOD = 9
EPS = 1e-6
NEG = -1e30
SCALE = HEAD_DIM ** -0.5

ROW_GROUP = 128
SUBLANES = 8
NORM_ROWS = 8
VMEM_LIMIT_BYTES = 56 * 1024 * 1024


def _params(semantics):
    return pltpu.CompilerParams(dimension_semantics=semantics,
                                vmem_limit_bytes=VMEM_LIMIT_BYTES)


def _slab_index(row_block, rows_per_block):
    return (row_block < M_PROMPT // rows_per_block).astype(jnp.int32)


def _part_specs(parts, tr, cols, col_index, grid_rank):
    specs = []
    for part in parts:
        row0, rows = part[1], part[2]
        b0, nb = row0 // tr, rows // tr
        blk = lambda i, b0=b0, nb=nb: jnp.clip(i - b0, 0, nb - 1)
        if _is_seq_major(part):
            assert part[0].shape[1] == (rows // DEC_BATCH) * cols
            shape = (DEC_BATCH, (tr // DEC_BATCH) * cols)
            imap = ((lambda i, blk=blk: (0, blk(i))) if grid_rank == 1 else
                    (lambda i, j, blk=blk: (0, blk(i))))
        else:
            shape = (tr, cols)
            imap = ((lambda i, blk=blk: (blk(i), col_index())) if grid_rank == 1 else
                    (lambda i, j, blk=blk: (blk(i), col_index(j))))
        mode = dict(pipeline_mode=pl.Buffered(1)) if nb == 1 else {}
        specs.append(pl.BlockSpec(shape, imap, **mode))
    return specs


def _is_seq_major(part):
    return len(part) > 3 and part[3]


def _read_part(ref, part, tr):
    v = ref[...]
    if _is_seq_major(part):
        steps = tr // DEC_BATCH
        cols = v.shape[1] // steps
        v = jnp.concatenate([v[:, t * cols:(t + 1) * cols] for t in range(steps)], axis=0)
    return v


def _pick_part(refs, parts, tr, row_block):
    v = _read_part(refs[0], parts[0], tr)
    for ref, part in zip(refs[1:], parts[1:]):
        v = jnp.where(row_block >= part[1] // tr, _read_part(ref, part, tr), v)
    return v


def _bucket_map():
    r = np.arange(BLOCK)[:, None]
    c = np.arange(2 * BLOCK)[None, :]
    n = np.maximum(BLOCK + r - c, 0)
    max_exact = N_BUCKETS // 2
    nf = np.maximum(n, 1).astype(np.float32)
    large = max_exact + (np.log(nf / max_exact) / math.log(MAX_DISTANCE / max_exact)
                         * (N_BUCKETS - max_exact)).astype(np.int32)
    large = np.minimum(large, N_BUCKETS - 1)
    return np.where(n < max_exact, n, large).astype(np.int32)


ADA_ROWS = ROW_GROUP + 16


def _ada_block(a_ref, w_ref, b_ref):
    r = jnp.dot(a_ref[...], w_ref[...].astype(BF16), preferred_element_type=F32) + b_ref[...]
    prompt = jnp.broadcast_to(r[ROW_GROUP:ROW_GROUP + 1], (ROW_GROUP, r.shape[1]))
    return jnp.concatenate([r[:ROW_GROUP], prompt], axis=0)


def _ada_kernel(c_ref, w_ref, b_ref, o_ref, a_ref):
    @pl.when(pl.program_id(0) == 0)
    def _():
        c = c_ref[...]
        a_ref[...] = (c / (1.0 + jnp.exp(-c))).astype(BF16)

    o_ref[...] = _ada_block(a_ref, w_ref, b_ref)


def _ada(c_rows, w_ada, b_ada, n, tn=512):
    return pl.pallas_call(
        _ada_kernel,
        out_shape=(jax.ShapeDtypeStruct((2 * ROW_GROUP, n), F32),
                   jax.ShapeDtypeStruct((ADA_ROWS, D_MODEL), BF16)),
        grid=(n // tn,),
        in_specs=[pl.BlockSpec((ADA_ROWS, D_MODEL), lambda j: (0, 0)),
                  pl.BlockSpec((D_MODEL, tn), lambda j: (0, j)),
                  pl.BlockSpec((1, tn), lambda j: (0, j))],
        out_specs=(pl.BlockSpec((2 * ROW_GROUP, tn), lambda j: (0, j)),
                   pl.BlockSpec((ADA_ROWS, D_MODEL), lambda j: (0, 0))),
        compiler_params=_params(("arbitrary",)),
        name="ada",
    )(c_rows, w_ada, b_ada)


def _norm_mod_kernel(*refs, parts, tr):
    n = len(parts)
    g_ref, sh_ref, sc_ref, o_ref = refs[n:]
    i = pl.program_id(0)
    for r0 in range(0, tr, NORM_ROWS):
        rows = slice(r0, r0 + NORM_ROWS)
        slab_rows = slice(r0 % ROW_GROUP, r0 % ROW_GROUP + NORM_ROWS)
        x = refs[0][rows, :]
        for ref, part in zip(refs[1:n], parts[1:]):
            x = jnp.where(i >= part[1] // tr, ref[rows, :], x)
        y = x * lax.rsqrt(jnp.mean(x * x, axis=-1, keepdims=True) + EPS) * g_ref[...]
        o_ref[rows, :] = (y * (1.0 + sc_ref[slab_rows, :]) + sh_ref[slab_rows, :]).astype(BF16)


def _norm_mod(parts, g, shift, scale):
    tr = 512 // len(parts)
    slab = lambda v: pl.BlockSpec((ROW_GROUP, D_MODEL), lambda i: (_slab_index(i, tr), v[1]))
    return pl.pallas_call(
        functools.partial(_norm_mod_kernel, parts=parts, tr=tr),
        out_shape=jax.ShapeDtypeStruct((M_ALL, D_MODEL), BF16),
        grid=(M_ALL // tr,),
        in_specs=_part_specs(parts, tr, D_MODEL, lambda: 0, 1) + [
            pl.BlockSpec((1, D_MODEL), lambda i: (0, 0)), slab(shift), slab(scale)],
        out_specs=pl.BlockSpec((tr, D_MODEL), lambda i: (i, 0)),
        compiler_params=_params(("parallel",)),
        name="norm_mod",
    )(*[p[0] for p in parts], g.reshape(1, D_MODEL), shift[0], scale[0])


def _final_norm_kernel(x_ref, g_ref, o_ref):
    x = x_ref[...]
    o_ref[...] = x * lax.rsqrt(jnp.mean(x * x, axis=-1, keepdims=True) + EPS) * g_ref[...]


def _final_norm(x, g, row0, rows, tr=512, seq_major_out=False):
    if seq_major_out:
        tr = DEC_BATCH
        out_shape = jax.ShapeDtypeStruct((DEC_BATCH, (rows // DEC_BATCH) * D_MODEL), F32)
        out_spec = pl.BlockSpec((tr, D_MODEL), lambda i: (0, i))
    else:
        out_shape = jax.ShapeDtypeStruct((rows, D_MODEL), F32)
        out_spec = pl.BlockSpec((tr, D_MODEL), lambda i: (i, 0))
    return pl.pallas_call(
        _final_norm_kernel,
        out_shape=out_shape,
        grid=(rows // tr,),
        in_specs=[pl.BlockSpec((tr, D_MODEL), lambda i: (i + row0 // tr, 0)),
                  pl.BlockSpec((1, D_MODEL), lambda i: (0, 0))],
        out_specs=out_spec,
        compiler_params=_params(("parallel",)),
        name="final_norm",
    )(x, g.reshape(1, D_MODEL))


def _resident_rows(tm, k, index_map):
    return pl.BlockSpec((tm, k), index_map, pipeline_mode=pl.Buffered(1))


def _grid_step():
    return pl.program_id(0) * pl.num_programs(1) + pl.program_id(1)


def _rounder_specs(rounders, n_inner, total_steps):
    in_specs, out_shapes, out_specs, static = [], [], [], []
    for src, col0, n_cols, cols, steps in rounders:
        assert total_steps >= steps and src.shape[0] % steps == 0 and n_cols % cols == 0
        rows = src.shape[0] // steps
        blk = lambda i, j, steps=steps: jnp.minimum(i * n_inner + j, steps - 1)
        in_specs.append(pl.BlockSpec((rows, src.shape[1]), lambda i, j, blk=blk: (blk(i, j), 0)))
        out_shapes.append(jax.ShapeDtypeStruct((n_cols // cols, src.shape[0], cols), BF16))
        out_specs.append(pl.BlockSpec((n_cols // cols, rows, cols),
                                      lambda i, j, blk=blk: (0, blk(i, j), 0)))
        static.append((col0, steps))
    return in_specs, out_shapes, out_specs, tuple(static)


def _run_rounders(static, src_refs, dst_refs):
    step = _grid_step()
    for (col0, steps), src_ref, dst_ref in zip(static, src_refs, dst_refs):
        @pl.when(step < steps)
        def _(col0=col0, src_ref=src_ref, dst_ref=dst_ref):
            cols = dst_ref.shape[2]
            for nb in range(dst_ref.shape[0]):
                lo = col0 + nb * cols
                dst_ref[nb] = src_ref[:, lo:lo + cols].astype(BF16)


def _mm_plain_kernel(x_ref, w_ref, o_ref):
    o_ref[...] = jnp.dot(x_ref[...], w_ref[...].astype(BF16), preferred_element_type=F32)


def _mm_plain(x, w, n, tm=1536, tn=512):
    m, k = x.shape
    return pl.pallas_call(
        _mm_plain_kernel,
        out_shape=jax.ShapeDtypeStruct((m, n), F32),
        grid=(m // tm, n // tn),
        in_specs=[pl.BlockSpec((tm, k), lambda i, j: (i, 0)),
                  pl.BlockSpec((k, tn), lambda i, j: (0, j))],
        out_specs=pl.BlockSpec((tm, tn), lambda i, j: (i, j)),
        compiler_params=_params(("parallel", "parallel")),
        name="mm_qkv",
    )(x, w)


def _mm_swiglu_kernel(*refs, rounders, ada_steps):
    n_r = len(rounders)
    n_ada = 3 if ada_steps else 0
    x_ref, w1_ref, w3_ref = refs[:3]
    n_in = 3 + n_ada + n_r
    o_ref = refs[n_in]
    x = x_ref[...]
    a = jnp.dot(x, w1_ref[...].astype(BF16), preferred_element_type=F32)
    b = jnp.dot(x, w3_ref[...].astype(BF16), preferred_element_type=F32)
    o_ref[...] = (a / (1.0 + jnp.exp(-a)) * b).astype(BF16)
    _run_rounders(rounders, refs[3 + n_ada:n_in], refs[len(refs) - n_r:])

    if ada_steps:
        @pl.when(_grid_step() < ada_steps)
        def _():
            refs[n_in + 1][...] = _ada_block(*refs[3:6])


W2_COLS = 512
ADA_COLS = 256


def _mm_swiglu(x, w1, w3, rounders=(), ada=None, tm=3072, tn=256):
    m, k = x.shape
    n = w1.shape[1]
    nj = n // tn
    steps = (m // tm) * nj
    in_specs = [_resident_rows(tm, k, lambda i, j: (i, 0)),
                pl.BlockSpec((k, tn), lambda i, j: (0, j)),
                pl.BlockSpec((k, tn), lambda i, j: (0, j))]
    out_shape = [jax.ShapeDtypeStruct((m, n), BF16)]
    out_specs = [pl.BlockSpec((tm, tn), lambda i, j: (i, j))]
    args = [x, w1, w3]
    ada_steps = 0
    if ada is not None:
        a, w_ada, b_ada, col0, n_ada = ada
        ada_steps = n_ada // ADA_COLS
        assert steps >= ada_steps
        ada_blk = lambda i, j: jnp.minimum(i * nj + j, ada_steps - 1)
        in_specs += [
            pl.BlockSpec(a.shape, lambda i, j: (0, 0), pipeline_mode=pl.Buffered(1)),
            pl.BlockSpec((k, ADA_COLS), lambda i, j: (0, col0 // ADA_COLS + ada_blk(i, j))),
            pl.BlockSpec((1, ADA_COLS), lambda i, j: (0, col0 // ADA_COLS + ada_blk(i, j)))]
        out_shape.append(jax.ShapeDtypeStruct((2 * ROW_GROUP, n_ada), F32))
        out_specs.append(pl.BlockSpec((2 * ROW_GROUP, ADA_COLS), lambda i, j: (0, ada_blk(i, j))))
        args += [a, w_ada, b_ada]
    r_in, r_shapes, r_out, r_static = _rounder_specs(rounders, nj, steps)
    return pl.pallas_call(
        functools.partial(_mm_swiglu_kernel, rounders=r_static, ada_steps=ada_steps),
        out_shape=(*out_shape, *r_shapes),
        grid=(m // tm, nj),
        in_specs=in_specs + r_in,
        out_specs=(*out_specs, *r_out),
        compiler_params=_params(("arbitrary", "arbitrary")),
        name="mm_swiglu",
    )(*args, *[r[0] for r in rounders])


def _mm_res_kernel(*refs, x_parts, res_parts, tm, coef):
    i = pl.program_id(0)
    pos = 0
    xs = []
    for parts in x_parts:
        xs.append(_pick_part(refs[pos:pos + len(parts)], parts, tm, i))
        pos += len(parts)
    w_ref = refs[pos]
    res = _pick_part(refs[pos + 1:pos + 1 + len(res_parts)], res_parts, tm, i)
    gate_ref, o_ref = refs[-2:]
    x = xs[0] if len(xs) == 1 else jnp.concatenate(xs, axis=1)
    acc = jnp.dot(x, w_ref[...].astype(BF16), preferred_element_type=F32)
    tn = acc.shape[1]
    gate = coef * gate_ref[...]
    y = acc.reshape(tm // ROW_GROUP, ROW_GROUP, tn) * gate[None]
    o_ref[...] = res + y.reshape(tm, tn)


def _mm_res(x_parts, w, res_parts, gate, coef, tm, tn, name):
    mod, k_gate = gate
    if w.ndim == 3:
        assert w.shape[2] == tn
        k, n = w.shape[1], w.shape[0] * tn
        w_spec = pl.BlockSpec((None, k, tn), lambda i, j: (j, 0, 0))
    else:
        k, n = w.shape
        w_spec = pl.BlockSpec((k, tn), lambda i, j: (0, j))
    gate_blocks = D_MODEL // tn
    in_specs, args = [], []
    for parts in x_parts:
        in_specs += _part_specs(parts, tm, parts[0][0].shape[1], lambda j: 0, 2)
        args += [p[0] for p in parts]
    in_specs.append(w_spec)
    in_specs += _part_specs(res_parts, tm, tn, lambda j: j, 2)
    in_specs.append(pl.BlockSpec(
        (ROW_GROUP, tn), lambda i, j: (_slab_index(i, tm), k_gate * gate_blocks + j)))
    args += [w] + [p[0] for p in res_parts] + [mod]
    return pl.pallas_call(
        functools.partial(_mm_res_kernel, x_parts=x_parts, res_parts=res_parts, tm=tm,
                          coef=coef),
        out_shape=jax.ShapeDtypeStruct((M_ALL, n), F32),
        grid=(M_ALL // tm, n // tn),
        in_specs=in_specs,
        out_specs=pl.BlockSpec((tm, tn), lambda i, j: (i, j)),
        compiler_params=_params(("parallel", "parallel")),
        name=name,
    )(*args)


CONV_COLS = 256
TAIL_ROWS = (CONV_WIDTH - 1) * ROW_GROUP


def _mm_conv_kernel(x_ref, wb_ref, wc_ref, wh_ref, cw_ref, st_ref, yc_ref, tail_ref,
                    carry_scr, *, sample):
    i = pl.program_id(0)
    c = pl.program_id(1)
    tm = x_ref.shape[0]

    if not sample:
        @pl.when(i == 0)
        def _():
            carry_scr[c] = jnp.zeros((SUBLANES, CONV_COLS), F32)

    x = x_ref[...]
    dot = lambda w_ref: jnp.dot(x, w_ref[...].astype(BF16), preferred_element_type=F32)
    u = dot(wc_ref) * dot(wh_ref)
    w0, w1, w2 = cw_ref[0:1, :], cw_ref[1:2, :], cw_ref[2:3, :]
    tail_ref[0] = u[tm - TAIL_ROWS:, :]
    if sample:
        shape = (DEC_SEQ, DEC_BATCH, CONV_COLS)
        u3 = u.reshape(shape)
        st = st_ref[...]
        u1 = jnp.concatenate([st[1:2], u3[:DEC_SEQ - 1]], axis=0)
        u2 = jnp.concatenate([st[0:2], u3[:DEC_SEQ - 2]], axis=0)
        y = (u2 * w0[None] + u1 * w1[None] + u3 * w2[None]).reshape(tm, CONV_COLS)
    else:
        hist = carry_scr[c]
        carry_scr[c] = u[tm - SUBLANES:, :]
        row = lax.broadcasted_iota(jnp.int32, (tm, CONV_COLS), 0)
        last, before_last = hist[SUBLANES - 1:SUBLANES, :], hist[SUBLANES - 2:SUBLANES - 1, :]
        u1 = jnp.where(row == 0, last, pltpu.roll(u, 1, axis=0))
        u2 = jnp.where(row == 0, before_last,
                       jnp.where(row == 1, last, pltpu.roll(u, 2, axis=0)))
        y = u2 * w0 + u1 * w1 + u * w2
    yc_ref[...] = (dot(wb_ref) * y).astype(BF16)


def _mm_conv(x, w_in, conv_w, state, row0, rows, sample, tm=M_SAMPLE):
    k = x.shape[1]
    nc = CONV_DIM // CONV_COLS
    first = QKV_DIM // CONV_COLS
    wcol = lambda base: pl.BlockSpec((k, CONV_COLS), lambda i, c: (0, first + base + c))
    return pl.pallas_call(
        functools.partial(_mm_conv_kernel, sample=sample),
        out_shape=(jax.ShapeDtypeStruct((rows, CONV_DIM), BF16),
                   jax.ShapeDtypeStruct((rows // tm, TAIL_ROWS, CONV_DIM), F32)),
        grid=(rows // tm, nc),
        in_specs=[pl.BlockSpec((tm, k), lambda i, c: (i + row0 // tm, 0)),
                  wcol(0), wcol(nc), wcol(2 * nc),
                  pl.BlockSpec((CONV_WIDTH, CONV_COLS), lambda i, c: (0, c)),
                  pl.BlockSpec((CONV_WIDTH - 1, DEC_BATCH, CONV_COLS), lambda i, c: (0, 0, c))],
        out_specs=(pl.BlockSpec((tm, CONV_COLS), lambda i, c: (i, c)),
                   pl.BlockSpec((1, TAIL_ROWS, CONV_COLS), lambda i, c: (i, 0, c))),
        scratch_shapes=[pltpu.VMEM((nc, SUBLANES, CONV_COLS), F32)],
        compiler_params=_params(("arbitrary", "arbitrary")),
        name="mm_conv",
    )(x, w_in, w_in, w_in, conv_w, state)


def _build_bias(tab_ref, bucket, bias_scr):
    for h in range(N_HEADS):
        acc = jnp.zeros(bucket.shape, F32)
        for b in range(N_BUCKETS):
            acc = jnp.where(bucket == b, tab_ref[b * N_HEADS + h], acc)
        bias_scr[h] = acc


def _valid_mask(rows, has_prev):
    r = lax.broadcasted_iota(jnp.int32, (rows, 2 * BLOCK), 0)
    c = lax.broadcasted_iota(jnp.int32, (rows, 2 * BLOCK), 1)
    dist = BLOCK + r - c
    return (dist >= 0) & (dist <= WINDOW) & ((c >= BLOCK) | has_prev)


def _softmax_with_sink(s, sink):
    m = jnp.maximum(jnp.max(s, axis=-1, keepdims=True), sink)
    p = jnp.exp(s - m)
    denom = jnp.sum(p, axis=-1, keepdims=True) + jnp.exp(sink - m)
    return p / denom


QUERY_BLOCKS = 4


def _attn_prompt_kernel(tab_ref, sink_ref, bucket_ref, q_ref, kc_ref, kp_ref, vc_ref, vp_ref,
                        o_ref, bias_scr):
    i = pl.program_id(0)

    @pl.when(i == 0)
    def _():
        _build_bias(tab_ref, bucket_ref[...], bias_scr)

    for g in range(N_KV_HEADS):
        kv = slice(g * HEAD_DIM, (g + 1) * HEAD_DIM)
        heads = [g * GQA_GROUP + j for j in range(GQA_GROUP)]
        for b in range(QUERY_BLOCKS):
            rows = slice(b * BLOCK, (b + 1) * BLOCK)
            if b == 0:
                k_prev, v_prev = kp_ref[:, kv], vp_ref[:, kv]
                valid = _valid_mask(BLOCK, i > 0)
            else:
                before = slice((b - 1) * BLOCK, b * BLOCK)
                k_prev, v_prev = kc_ref[before, kv], vc_ref[before, kv]
                valid = _valid_mask(BLOCK, True)
            qs = jnp.concatenate(
                [q_ref[rows, h * HEAD_DIM:(h + 1) * HEAD_DIM] for h in heads], axis=0).astype(BF16)
            kk = jnp.concatenate([k_prev, kc_ref[rows, kv]], axis=0).astype(BF16)
            vv = jnp.concatenate([v_prev, vc_ref[rows, kv]], axis=0).astype(BF16)
            s = lax.dot_general(qs, kk, (((1,), (1,)), ((), ())), preferred_element_type=F32)
            ps = []
            for j, h in enumerate(heads):
                sj = s[j * BLOCK:(j + 1) * BLOCK] * SCALE + bias_scr[h]
                sj = jnp.where(valid, sj, NEG)
                ps.append(_softmax_with_sink(sj, sink_ref[h]))
            p = jnp.concatenate(ps, axis=0).astype(BF16)
            o = jnp.dot(p, vv, preferred_element_type=F32)
            for j, h in enumerate(heads):
                o_ref[rows, h * HEAD_DIM:(h + 1) * HEAD_DIM] = (
                    o[j * BLOCK:(j + 1) * BLOCK].astype(BF16))


def _attn_prompt(qkv, table, sinks, bucket):
    rows = QUERY_BLOCKS * BLOCK
    q_blk = 0
    k_blk = ATTN_DIM // KV_DIM
    v_blk = k_blk + 1
    prev = lambda i: jnp.maximum(QUERY_BLOCKS * i - 1, 0)
    smem = pl.BlockSpec(memory_space=pltpu.SMEM)
    return pl.pallas_call(
        _attn_prompt_kernel,
        out_shape=jax.ShapeDtypeStruct((M_PROMPT, ATTN_DIM), BF16),
        grid=(M_PROMPT // rows,),
        in_specs=[smem, smem,
                  pl.BlockSpec((BLOCK, 2 * BLOCK), lambda i: (0, 0)),
                  pl.BlockSpec((rows, ATTN_DIM), lambda i: (i, q_blk)),
                  pl.BlockSpec((rows, KV_DIM), lambda i: (i, k_blk)),
                  pl.BlockSpec((BLOCK, KV_DIM), lambda i: (prev(i), k_blk)),
                  pl.BlockSpec((rows, KV_DIM), lambda i: (i, v_blk)),
                  pl.BlockSpec((BLOCK, KV_DIM), lambda i: (prev(i), v_blk))],
        out_specs=pl.BlockSpec((rows, ATTN_DIM), lambda i: (i, 0)),
        scratch_shapes=[pltpu.VMEM((N_HEADS, BLOCK, 2 * BLOCK), F32)],
        compiler_params=_params(("arbitrary",)),
        name="attn_prompt",
    )(table, sinks, bucket, qkv, qkv, qkv, qkv, qkv)


def _attn_sample_kernel(tab_ref, sink_ref, bucket_ref, q_ref, kn_ref, vn_ref, ck_ref, cv_ref,
                        o_ref, kw_ref, vw_ref, bias_scr):
    i = pl.program_id(0)
    g_seq = q_ref.shape[0]
    wb = ck_ref.shape[1] // N_KV_HEADS

    @pl.when(i == 0)
    def _():
        _build_bias(tab_ref, bucket_ref[0:DEC_SEQ, :], bias_scr)

    valid = _valid_mask(DEC_SEQ, True)[None]
    pad = jnp.zeros((g_seq, 2 * BLOCK - wb - DEC_SEQ, HEAD_DIM), F32)
    for g in range(N_KV_HEADS):
        kv = slice(g * HEAD_DIM, (g + 1) * HEAD_DIM)
        head_rows = pl.ds(g, wb, stride=N_KV_HEADS)
        heads = [g * GQA_GROUP + j for j in range(GQA_GROUP)]
        qs = jnp.concatenate(
            [q_ref[:, :, h * HEAD_DIM:(h + 1) * HEAD_DIM] for h in heads], axis=1).astype(BF16)
        kk = jnp.concatenate([ck_ref[:, head_rows, :], kn_ref[:, :, kv], pad], axis=1).astype(BF16)
        vv = jnp.concatenate([cv_ref[:, head_rows, :], vn_ref[:, :, kv], pad], axis=1).astype(BF16)
        s = jnp.einsum('gqd,gkd->gqk', qs, kk, preferred_element_type=F32)
        ps = []
        for j, h in enumerate(heads):
            sj = s[:, j * DEC_SEQ:(j + 1) * DEC_SEQ] * SCALE + bias_scr[h][None]
            sj = jnp.where(valid, sj, NEG)
            ps.append(_softmax_with_sink(sj, sink_ref[h]))
        p = jnp.concatenate(ps, axis=1).astype(BF16)
        o = jnp.einsum('gqk,gkd->gqd', p, vv, preferred_element_type=F32)
        for j, h in enumerate(heads):
            o_ref[:, :, h * HEAD_DIM:(h + 1) * HEAD_DIM] = (
                o[:, j * DEC_SEQ:(j + 1) * DEC_SEQ].astype(BF16))
        new_rows = pl.ds((wb - DEC_SEQ) * N_KV_HEADS + g, DEC_SEQ, stride=N_KV_HEADS)
        kw_ref[:, new_rows, :] = kn_ref[:, :, kv]
        vw_ref[:, new_rows, :] = vn_ref[:, :, kv]
    kept = (wb - DEC_SEQ) * N_KV_HEADS
    kw_ref[:, 0:kept, :] = ck_ref[:, DEC_SEQ * N_KV_HEADS:, :]
    vw_ref[:, 0:kept, :] = cv_ref[:, DEC_SEQ * N_KV_HEADS:, :]


def _attn_sample(q, kn, vn, ck, cv, table, sinks, bucket, g_seq=16):
    rows = ck.shape[1]
    smem = pl.BlockSpec(memory_space=pltpu.SMEM)
    blk = lambda r, c: pl.BlockSpec((g_seq, r, c), lambda i: (i, 0, 0))
    return pl.pallas_call(
        _attn_sample_kernel,
        out_shape=(jax.ShapeDtypeStruct((DEC_BATCH, DEC_SEQ, ATTN_DIM), BF16),
                   jax.ShapeDtypeStruct((DEC_BATCH, rows, HEAD_DIM), F32),
                   jax.ShapeDtypeStruct((DEC_BATCH, rows, HEAD_DIM), F32)),
        grid=(DEC_BATCH // g_seq,),
        in_specs=[smem, smem,
                  pl.BlockSpec((BLOCK, 2 * BLOCK), lambda i: (0, 0)),
                  blk(DEC_SEQ, ATTN_DIM), blk(DEC_SEQ, KV_DIM), blk(DEC_SEQ, KV_DIM),
                  blk(rows, HEAD_DIM), blk(rows, HEAD_DIM)],
        out_specs=(blk(DEC_SEQ, ATTN_DIM), blk(rows, HEAD_DIM), blk(rows, HEAD_DIM)),
        scratch_shapes=[pltpu.VMEM((N_HEADS, DEC_SEQ, 2 * BLOCK), F32)],
        compiler_params=_params(("arbitrary",)),
        name="attn_sample",
    )(table, sinks, bucket, q, kn, vn, ck, cv)


def _to_step_major(a):
    return a.transpose(1, 0, 2).reshape(M_SAMPLE, a.shape[-1])


def _to_seq_major(a):
    return a.reshape(DEC_SEQ, DEC_BATCH, a.shape[-1]).transpose(1, 0, 2)


def _whole(a):
    return [(a, 0, M_ALL)]


N_MOD_EARLY = 2


def _ffn(x_parts, mods, k0, g, w1, w3, w2, ada_rest=None):
    h = _norm_mod(x_parts, g, mods[k0], mods[k0 + 1])
    sweep = w1.shape[1] // 256
    if ada_rest is None:
        u, w2_bf16 = _mm_swiglu(
            h, w1, w3, rounders=[(w2, 0, D_MODEL, W2_COLS, 2 * sweep)])
    else:
        n_rest = (N_MOD - N_MOD_EARLY) * D_MODEL
        u, mod_rest, w2_bf16 = _mm_swiglu(
            h, w1, w3, rounders=[(w2, 0, D_MODEL, W2_COLS, 4 * sweep)],
            ada=ada_rest + (N_MOD_EARLY * D_MODEL, n_rest), tm=2304)
        mods += [(mod_rest, k) for k in range(N_MOD - N_MOD_EARLY)]
    return _mm_res([_whole(u)], w2_bf16, x_parts, mods[k0 + 2], 0.5, tm=512, tn=W2_COLS,
                   name="mm_w2")


def kernel(x_prompt, x_sample, c_prompt, c_sample, cache_k, cache_v, state_conv, rel_bias,
           g_ffn1, w1_ffn1, w3_ffn1, w2_ffn1, g_mix, w_in, sinks, conv_w, w_out,
           g_ffn2, w1_ffn2, w3_ffn2, w2_ffn2, w_ada, b_ada, g_final):
    depth = w_in.shape[0]
    wb = cache_k.shape[2]
    bucket = jnp.asarray(_bucket_map())
    table = rel_bias.reshape(N_BUCKETS * N_HEADS)
    c_rows = jnp.concatenate(
        [c_sample, c_prompt, jnp.zeros((ADA_ROWS - ROW_GROUP - 1, D_MODEL), F32)], axis=0)

    x_parts = [(x_prompt.reshape(M_PROMPT, D_MODEL), 0, M_PROMPT),
               (_to_step_major(x_sample), M_PROMPT, M_SAMPLE)]

    kp, vp, cp, ks, vs, cs = [], [], [], [], [], []
    for l in range(depth):
        b_ada_l = b_ada[l].reshape(1, N_MOD * D_MODEL)
        mod_early, silu_c = _ada(c_rows, w_ada[l], b_ada_l, N_MOD_EARLY * D_MODEL)
        mods = [(mod_early, k) for k in range(N_MOD_EARLY)]
        x = _ffn(x_parts, mods, 0, g_ffn1[l], w1_ffn1[l], w3_ffn1[l], w2_ffn1[l],
                 ada_rest=(silu_c, w_ada[l], b_ada_l))

        h = _norm_mod(_whole(x), g_mix[l], mods[3], mods[4])
        qkv = _mm_plain(h, w_in[l], QKV_DIM)
        conv_args = (h, w_in[l], conv_w[l], state_conv[l].transpose(1, 0, 2))
        yc_p, tails_p = _mm_conv(*conv_args, 0, M_PROMPT, False)
        yc_s, tails_s = _mm_conv(*conv_args, M_PROMPT, M_SAMPLE, True)
        yc_parts = [(yc_p, 0, M_PROMPT), (yc_s, M_PROMPT, M_SAMPLE)]

        o_p = _attn_prompt(qkv, table, sinks[l], bucket)
        qkv_s = qkv[M_PROMPT:]
        q_s = _to_seq_major(qkv_s[:, :ATTN_DIM])
        kn_s = _to_seq_major(qkv_s[:, ATTN_DIM:ATTN_DIM + KV_DIM])
        vn_s = _to_seq_major(qkv_s[:, ATTN_DIM + KV_DIM:])
        o_s, kw_s, vw_s = _attn_sample(
            q_s, kn_s, vn_s,
            cache_k[l].reshape(DEC_BATCH, wb * N_KV_HEADS, HEAD_DIM),
            cache_v[l].reshape(DEC_BATCH, wb * N_KV_HEADS, HEAD_DIM), table, sinks[l], bucket)
        o_parts = [(o_p, 0, M_PROMPT),
                   (o_s.reshape(DEC_BATCH, DEC_SEQ * ATTN_DIM), M_PROMPT, M_SAMPLE, True)]
        x = _mm_res([o_parts, yc_parts], w_out[l], _whole(x), mods[5], 1.0, tm=1024,
                    tn=512, name="mm_out")

        x = _ffn(_whole(x), mods, 6, g_ffn2[l], w1_ffn2[l], w3_ffn2[l], w2_ffn2[l])
        x_parts = _whole(x)

        k_p = qkv[M_PROMPT - wb:M_PROMPT, ATTN_DIM:ATTN_DIM + KV_DIM]
        v_p = qkv[M_PROMPT - wb:M_PROMPT, ATTN_DIM + KV_DIM:]
        kp.append(k_p.reshape(1, wb, N_KV_HEADS, HEAD_DIM))
        vp.append(v_p.reshape(1, wb, N_KV_HEADS, HEAD_DIM))
        cp.append(tails_p[-1, TAIL_ROWS - (CONV_WIDTH - 1):].reshape(
            1, CONV_WIDTH - 1, CONV_DIM))
        ks.append(kw_s.reshape(DEC_BATCH, wb, N_KV_HEADS, HEAD_DIM))
        vs.append(vw_s.reshape(DEC_BATCH, wb, N_KV_HEADS, HEAD_DIM))
        cs.append(tails_s[0].reshape(
            CONV_WIDTH - 1, DEC_BATCH, CONV_DIM).transpose(1, 0, 2))

    y_prompt = _final_norm(x, g_final, 0, M_PROMPT).reshape(1, SEQ, D_MODEL)
    y_sample = _final_norm(x, g_final, M_PROMPT, M_SAMPLE, seq_major_out=True).reshape(
        DEC_BATCH, DEC_SEQ, D_MODEL)
    return (y_prompt, y_sample, jnp.stack(kp), jnp.stack(vp), jnp.stack(cp),
            jnp.stack(ks), jnp.stack(vs), jnp.stack(cs))
```
